```python
import math
import jax, jax.numpy as jnp
from jax import lax
import numpy as np

D_MODEL = 1024
BATCH = 16
SEQ = 2048
DEPTH = 2
DEC_BATCH = 4
DEC_SEQ = 8192
PAST_LEN = 128

EPS = 1e-6
ROPE_THETA = 500000.0
Q_BLOCK = 128
N_MIXERS = 2
N_A_LAYERS = (DEPTH + 1) // 2
N_B_LAYERS = DEPTH // 2

DIFF_HEADS = 8
DIFF_HEAD_DIM = D_MODEL // DIFF_HEADS // 2
DIFF_V_DIM = 2 * DIFF_HEAD_DIM
DIFF_QK_W = DIFF_HEADS * 2 * DIFF_HEAD_DIM
DIFF_V_W = DIFF_HEADS * DIFF_V_DIM
DIFF_ROT_DIM = DIFF_HEAD_DIM // 4

MLA_HEADS = 16
MLA_NOPE = 64
MLA_ROPE = 32
MLA_V = 64
MLA_Q_RANK = 384
MLA_KV_RANK = 256
MLA_A_W = MLA_Q_RANK + MLA_KV_RANK + MLA_ROPE
MLA_QK = MLA_NOPE + MLA_ROPE

N_EXPERTS = 16
CAPACITY_FACTOR = 2
D_FF = 2048

kernel_name = "diffattn_mla_expert_choice_encoder"


def _rmsnorm(x, g):
    xf = x.astype(jnp.float32)
    y = xf * lax.rsqrt(jnp.mean(xf * xf, axis=-1, keepdims=True) + EPS)
    return (y * g.astype(jnp.float32)).astype(x.dtype)


def _rope(x):
    s, r = x.shape[1], x.shape[-1]
    pos = jnp.arange(s, dtype=jnp.float32)
    inv = jnp.float32(ROPE_THETA) ** (-jnp.arange(0, r, 2, dtype=jnp.float32) / r)
    ang = pos[:, None] * inv[None, :]
    cos = jnp.cos(ang)[None, :, None, :]
    sin = jnp.sin(ang)[None, :, None, :]
    xf = x.astype(jnp.float32)
    x1, x2 = xf[..., : r // 2], xf[..., r // 2:]
    return jnp.concatenate([x1 * cos - x2 * sin, x2 * cos + x1 * sin], axis=-1).astype(x.dtype)


def _partial_rope(x, rot_dim):
    return jnp.concatenate([_rope(x[..., :rot_dim]), x[..., rot_dim:]], axis=-1)


def _map_query_blocks(fn, q):
    b, s = q.shape[0], q.shape[1]
    nb = s // Q_BLOCK
    qb = jnp.moveaxis(q.reshape((b, nb, Q_BLOCK) + q.shape[2:]), 1, 0)
    out = lax.map(fn, qb)
    out = jnp.moveaxis(out, 0, 1)
    return out.reshape((b, s) + out.shape[3:])


def _diff_attention(h, w_qkv, lq1, lk1, lq2, lk2, subln_g, w_o, lambda_init):
    b, s, _ = h.shape
    qkv = h @ w_qkv
    q, k, v = qkv[..., :DIFF_QK_W], qkv[..., DIFF_QK_W:2 * DIFF_QK_W], qkv[..., 2 * DIFF_QK_W:]
    q = _partial_rope(q.reshape(b, s, 2 * DIFF_HEADS, DIFF_HEAD_DIM), DIFF_ROT_DIM)
    k = _partial_rope(k.reshape(b, s, 2 * DIFF_HEADS, DIFF_HEAD_DIM), DIFF_ROT_DIM)
    q = q.reshape(b, s, DIFF_HEADS, 2, DIFF_HEAD_DIM)
    k = k.reshape(b, s, DIFF_HEADS, 2, DIFF_HEAD_DIM)
    v = v.reshape(b, s, DIFF_HEADS, DIFF_V_DIM)
    f32 = jnp.float32
    lam = (jnp.exp(jnp.sum(lq1.astype(f32) * lk1.astype(f32)))
           - jnp.exp(jnp.sum(lq2.astype(f32) * lk2.astype(f32))) + lambda_init)
    scale = DIFF_HEAD_DIM ** -0.5

    def block(qb):
        sc = jnp.einsum('bqhcd,bkhcd->bhcqk', qb, k, preferred_element_type=f32) * scale
        p = jax.nn.softmax(sc, axis=-1)
        a = p[:, :, 0] - lam * p[:, :, 1]
        return jnp.einsum('bhqk,bkhe->bqhe', a, v.astype(f32)).astype(h.dtype)

    o = _map_query_blocks(block, q)
    o = _rmsnorm(o, subln_g) * (1.0 - lambda_init)
    return o.reshape(b, s, DIFF_V_W) @ w_o


def _mla(h, w_a, q_norm_g, w_q_b, kv_norm_g, w_kv_b, w_o):
    b, s, _ = h.shape
    a = h @ w_a
    cq = _rmsnorm(a[..., :MLA_Q_RANK], q_norm_g)
    ckv = _rmsnorm(a[..., MLA_Q_RANK:MLA_Q_RANK + MLA_KV_RANK], kv_norm_g)
    k_rope = _rope(a[..., MLA_Q_RANK + MLA_KV_RANK:][:, :, None, :])[:, :, 0, :]
    q = (cq @ w_q_b).reshape(b, s, MLA_HEADS, MLA_QK)
    q = jnp.concatenate([q[..., :MLA_NOPE], _rope(q[..., MLA_NOPE:])], axis=-1)
    kv = (ckv @ w_kv_b).reshape(b, s, MLA_HEADS, MLA_NOPE + MLA_V)
    k_nope, v = kv[..., :MLA_NOPE], kv[..., MLA_NOPE:]
    f32 = jnp.float32
    scale = MLA_QK ** -0.5

    def block(qb):
        sc = (jnp.einsum('bqhd,bkhd->bhqk', qb[..., :MLA_NOPE], k_nope, preferred_element_type=f32)
              + jnp.einsum('bqhr,bkr->bhqk', qb[..., MLA_NOPE:], k_rope, preferred_element_type=f32)) * scale
        p = jax.nn.softmax(sc, axis=-1)
        return jnp.einsum('bhqk,bkhe->bqhe', p, v.astype(f32)).astype(h.dtype)

    o = _map_query_blocks(block, q)
    return o.reshape(b, s, MLA_HEADS * MLA_V) @ w_o


def _expert_choice_ffn(h, router_w, w_gate, w_up, w_down):
    b, s, d = h.shape
    t = b * s
    cap = max(1, CAPACITY_FACTOR * t // N_EXPERTS)
    ht = h.reshape(t, d)
    aff = jax.nn.softmax(ht.astype(jnp.float32) @ router_w.astype(jnp.float32), axis=-1)
    gates, idx = lax.top_k(aff.T, cap)
    xe = ht[idx]
    hid = jax.nn.silu(jnp.einsum('ecd,edf->ecf', xe, w_gate)) * jnp.einsum('ecd,edf->ecf', xe, w_up)
    ye = jnp.einsum('ecf,efd->ecd', hid, w_down) * gates[..., None].astype(h.dtype)
    out = jnp.zeros_like(ht).at[idx.reshape(-1)].add(ye.reshape(-1, d))
    return out.reshape(b, s, d)


def _trunk(x, diff_norm_g, diff_w_qkv, diff_lambda_q1, diff_lambda_k1, diff_lambda_q2, diff_lambda_k2,
           diff_subln_g, diff_w_o, mla_norm_g, mla_w_a, mla_q_norm_g, mla_w_q_b, mla_kv_norm_g, mla_w_kv_b,
           mla_w_o, ffn_norm_g, router_w, w_gate, w_up, w_down, final_norm_g):
    for i in range(DEPTH):
        j = i // N_MIXERS
        if i % N_MIXERS == 0:
            lambda_init = 0.8 - 0.6 * math.exp(-0.3 * i)
            x = x + _diff_attention(_rmsnorm(x, diff_norm_g[j]), diff_w_qkv[j], diff_lambda_q1[j],
                                    diff_lambda_k1[j], diff_lambda_q2[j], diff_lambda_k2[j],
                                    diff_subln_g[j], diff_w_o[j], lambda_init)
        else:
            x = x + _mla(_rmsnorm(x, mla_norm_g[j]), mla_w_a[j], mla_q_norm_g[j], mla_w_q_b[j],
                         mla_kv_norm_g[j], mla_w_kv_b[j], mla_w_o[j])
        x = x + _expert_choice_ffn(_rmsnorm(x, ffn_norm_g[i]), router_w[i], w_gate[i], w_up[i], w_down[i])
    return _rmsnorm(x, final_norm_g)


def setup_inputs(seed: int = 0) -> dict:
    key = jax.random.key(seed)
    ks = jax.random.split(key, 24)
    f32 = jnp.float32

    def nrm(k, shape, scale):
        return jax.random.normal(k, shape, f32) * scale

    def gain(k, shape):
        return 1.0 + 0.01 * jax.random.normal(k, shape, f32)

    na, nb, D = N_A_LAYERS, N_B_LAYERS, D_MODEL
    return {
        "x_prompt": nrm(ks[0], (BATCH, SEQ, D), 1.0),
        "x_sample": nrm(ks[1], (DEC_BATCH, DEC_SEQ, D), 1.0),
        "diff_norm_g": gain(ks[2], (na, D)),
        "diff_w_qkv": nrm(ks[3], (na, D, 2 * DIFF_QK_W + DIFF_V_W), D ** -0.5),
        "diff_lambda_q1": nrm(ks[4], (na, DIFF_HEAD_DIM), 0.1),
        "diff_lambda_k1": nrm(ks[5], (na, DIFF_HEAD_DIM), 0.1),
        "diff_lambda_q2": nrm(ks[6], (na, DIFF_HEAD_DIM), 0.1),
        "diff_lambda_k2": nrm(ks[7], (na, DIFF_HEAD_DIM), 0.1),
        "diff_subln_g": gain(ks[8], (na, DIFF_V_DIM)),
        "diff_w_o": nrm(ks[9], (na, DIFF_V_W, D), DIFF_V_W ** -0.5),
        "mla_norm_g": gain(ks[10], (nb, D)),
        "mla_w_a": nrm(ks[11], (nb, D, MLA_A_W), D ** -0.5),
        "mla_q_norm_g": gain(ks[12], (nb, MLA_Q_RANK)),
        "mla_w_q_b": nrm(ks[13], (nb, MLA_Q_RANK, MLA_HEADS * MLA_QK), MLA_Q_RANK ** -0.5),
        "mla_kv_norm_g": gain(ks[14], (nb, MLA_KV_RANK)),
        "mla_w_kv_b": nrm(ks[15], (nb, MLA_KV_RANK, MLA_HEADS * (MLA_NOPE + MLA_V)), MLA_KV_RANK ** -0.5),
        "mla_w_o": nrm(ks[16], (nb, MLA_HEADS * MLA_V, D), (MLA_HEADS * MLA_V) ** -0.5),
        "ffn_norm_g": gain(ks[17], (DEPTH, D)),
        "router_w": nrm(ks[18], (DEPTH, D, N_EXPERTS), D ** -0.5),
        "w_gate": nrm(ks[19], (DEPTH, N_EXPERTS, D, D_FF), D ** -0.5),
        "w_up": nrm(ks[20], (DEPTH, N_EXPERTS, D, D_FF), D ** -0.5),
        "w_down": nrm(ks[21], (DEPTH, N_EXPERTS, D_FF, D), D_FF ** -0.5),
        "final_norm_g": gain(ks[22], (D,)),
    }


def reference(x_prompt, x_sample, diff_norm_g, diff_w_qkv, diff_lambda_q1, diff_lambda_k1, diff_lambda_q2,
              diff_lambda_k2, diff_subln_g, diff_w_o, mla_norm_g, mla_w_a, mla_q_norm_g, mla_w_q_b,
              mla_kv_norm_g, mla_w_kv_b, mla_w_o, ffn_norm_g, router_w, w_gate, w_up, w_down, final_norm_g):
    y_prompt = _trunk(x_prompt, diff_norm_g, diff_w_qkv, diff_lambda_q1, diff_lambda_k1, diff_lambda_q2,
                      diff_lambda_k2, diff_subln_g, diff_w_o, mla_norm_g, mla_w_a, mla_q_norm_g, mla_w_q_b,
                      mla_kv_norm_g, mla_w_kv_b, mla_w_o, ffn_norm_g, router_w, w_gate, w_up, w_down,
                      final_norm_g)
    y_sample = _trunk(x_sample, diff_norm_g, diff_w_qkv, diff_lambda_q1, diff_lambda_k1, diff_lambda_q2,
                      diff_lambda_k2, diff_subln_g, diff_w_o, mla_norm_g, mla_w_a, mla_q_norm_g, mla_w_q_b,
                      mla_kv_norm_g, mla_w_kv_b, mla_w_o, ffn_norm_g, router_w, w_gate, w_up, w_down,
                      final_norm_g)
    return (y_prompt, y_sample)
```

```python
import functools

import numpy as np
import jax
import jax.numpy as jnp
from jax import lax
from jax.experimental import pallas as pl
from jax.experimental.pallas import tpu as pltpu

_F32, _BF16, _I32 = jnp.float32, jnp.bfloat16, jnp.int32
_EPS = 1e-6
_ROPE_THETA = 500000.0
_LANES = 128
_NEG = -1e30
_VMEM_LIMIT = 56 * 1024 * 1024

_DIFF_HEADS, _DIFF_HEAD_DIM, _DIFF_ROT = 8, 64, 16
_MLA_HEADS, _MLA_NOPE, _MLA_ROPE, _MLA_V = 16, 64, 32, 64
_MLA_Q_RANK, _MLA_KV_RANK = 384, 256
_N_EXPERTS, _CAPACITY_FACTOR = 16, 2
_N_MIXERS = 2


def _cparams(*sem):
    return pltpu.CompilerParams(dimension_semantics=sem, vmem_limit_bytes=_VMEM_LIMIT)


def _tile(n, pref):
    t = min(n, pref)
    assert n % t == 0, (n, pref)
    return t


def _rmsnorm(x, g):
    return x * lax.rsqrt(jnp.mean(x * x, axis=-1, keepdims=True) + _EPS) * g


def _rope_lanes(y, c, s_up, s_dn, half):
    return y * c + pltpu.roll(y, _LANES - half, 1) * s_up + pltpu.roll(y, half, 1) * s_dn


def _rope_tables(seq, group, start, rot):
    half = rot // 2
    pos = jnp.arange(seq, dtype=_F32)
    inv = jnp.float32(_ROPE_THETA) ** (-jnp.arange(0, rot, 2, dtype=_F32) / rot)
    ang = pos[:, None] * inv[None, :]
    cos, sin = jnp.cos(ang), jnp.sin(ang)
    j = (np.arange(_LANES) % group) - start
    first = (j >= 0) & (j < half)
    second = (j >= half) & (j < rot)
    f = np.where(first, j, np.where(second, j - half, 0))
    c = jnp.where(first | second, cos[:, f], 1.0)
    s_up = jnp.where(first, -sin[:, f], 0.0)
    s_dn = jnp.where(second, sin[:, f], 0.0)
    return c, s_up, s_dn


def _flash_rows(q, k_ref, v_ref, tk):
    rows = q.shape[0]
    n_chunks = k_ref.shape[0] // tk

    def body(j, carry):
        m, l, acc = carry
        off = pl.multiple_of(j * tk, tk)
        ks = k_ref[pl.ds(off, tk), :]
        vs = v_ref[pl.ds(off, tk), :]
        s = lax.dot_general(q, ks, (((1,), (1,)), ((), ())), preferred_element_type=_F32)
        m_new = jnp.maximum(m, jnp.max(s, axis=-1, keepdims=True))
        alpha = jnp.exp(m - m_new)
        p = jnp.exp(s - m_new)
        l = alpha * l + jnp.sum(p, axis=-1, keepdims=True)
        acc = alpha * acc + jnp.dot(p.astype(_BF16), vs, preferred_element_type=_F32)
        return m_new, l, acc

    init = (jnp.full((rows, 1), _NEG, _F32), jnp.zeros((rows, 1), _F32),
            jnp.zeros((rows, v_ref.shape[1]), _F32))
    _, l, acc = lax.fori_loop(0, n_chunks, body, init)
    return acc * (1.0 / l)


def _diff_qkv_kernel(x_ref, g_ref, w_ref, c_ref, su_ref, sd_ref, q_ref, k_ref, v_ref, *, scale):
    h = _rmsnorm(x_ref[...], g_ref[...]).astype(_BF16)
    y = jnp.dot(h, w_ref[...], preferred_element_type=_F32)
    d = q_ref.shape[-1]
    c, su, sd = c_ref[...], su_ref[...], sd_ref[...]
    half = _DIFF_ROT // 2
    for j in range(d // _LANES):
        lo, hi = j * _LANES, (j + 1) * _LANES
        q_ref[:, lo:hi] = (_rope_lanes(y[:, lo:hi], c, su, sd, half) * scale).astype(_BF16)
        k_ref[:, lo:hi] = _rope_lanes(y[:, d + lo:d + hi], c, su, sd, half).astype(_BF16)
    v_ref[...] = y[:, 2 * d:].astype(_BF16)


def _diff_qkv(x, g, w, seq):
    t, d = x.shape
    tm = _tile(seq, 512)
    c, su, sd = _rope_tables(seq, _DIFF_HEAD_DIM, 0, _DIFF_ROT)
    nseq = seq // tm
    tab = pl.BlockSpec((tm, _LANES), lambda i: (i % nseq, 0))
    row = pl.BlockSpec((tm, d), lambda i: (i, 0))
    out = jax.ShapeDtypeStruct((t, d), _BF16)
    return pl.pallas_call(
        functools.partial(_diff_qkv_kernel, scale=_DIFF_HEAD_DIM ** -0.5),
        grid=(t // tm,),
        in_specs=[row, pl.BlockSpec((1, d), lambda i: (0, 0)),
                  pl.BlockSpec((d, 3 * d), lambda i: (0, 0)), tab, tab, tab],
        out_specs=[row, row, row],
        out_shape=[out, out, out],
        compiler_params=_cparams("parallel"),
        name="diff_qkv",
    )(x, g.reshape(1, d), w.astype(_BF16), c, su, sd)


def _diff_attn_kernel(q_ref, k_ref, v_ref, lq1_ref, lk1_ref, lq2_ref, lk2_ref, g_ref, o_ref,
                      *, lambda_init, tk):
    q = q_ref[...]
    tq = q.shape[0]
    lane = lax.broadcasted_iota(_I32, q.shape, 1)
    zero = jnp.zeros_like(q)
    qq = jnp.concatenate([jnp.where(lane < _DIFF_HEAD_DIM, q, zero),
                          jnp.where(lane >= _DIFF_HEAD_DIM, q, zero)], axis=0)
    o = _flash_rows(qq, k_ref, v_ref, tk)
    lam = (jnp.exp(jnp.sum(lq1_ref[...] * lk1_ref[...], axis=-1, keepdims=True))
           - jnp.exp(jnp.sum(lq2_ref[...] * lk2_ref[...], axis=-1, keepdims=True)) + lambda_init)
    o = o[:tq] - lam * o[tq:]
    o = _rmsnorm(o, g_ref[...]) * (1.0 - lambda_init)
    o_ref[...] = o.astype(_BF16)


def _diff_attn(q, k, v, lq1, lk1, lq2, lk2, subln_g, batch, seq, lambda_init):
    t, d = q.shape
    tq = _tile(seq, 128)
    tk = _tile(seq, 512)
    nq = seq // tq
    qspec = pl.BlockSpec((tq, _LANES), lambda b, h, i: (b * nq + i, h))
    kvspec = pl.BlockSpec((seq, _LANES), lambda b, h, i: (b, h))
    small = lambda n: pl.BlockSpec((1, n), lambda b, h, i: (0, 0))
    hd = _DIFF_HEAD_DIM
    return pl.pallas_call(
        functools.partial(_diff_attn_kernel, lambda_init=lambda_init, tk=tk),
        grid=(batch, _DIFF_HEADS, nq),
        in_specs=[qspec, kvspec, kvspec, small(hd), small(hd), small(hd), small(hd), small(2 * hd)],
        out_specs=qspec,
        out_shape=jax.ShapeDtypeStruct((t, d), _BF16),
        compiler_params=_cparams("parallel", "parallel", "arbitrary"),
        name="diff_attn",
    )(q, k, v, lq1.reshape(1, hd), lk1.reshape(1, hd), lq2.reshape(1, hd), lk2.reshape(1, hd),
      subln_g.reshape(1, 2 * hd))


def _mla_proj_kernel(x_ref, g_ref, wa_ref, gq_ref, wq_ref, gkv_ref, wkv_ref, c_ref, su_ref, sd_ref,
                     q_ref, k_ref, v_ref, *, scale):
    h = _rmsnorm(x_ref[...], g_ref[...]).astype(_BF16)
    a = jnp.dot(h, wa_ref[...], preferred_element_type=_F32)
    c, su, sd = c_ref[...], su_ref[...], sd_ref[...]
    half = _MLA_ROPE // 2
    kv_lo = _MLA_Q_RANK + _MLA_KV_RANK
    cq = _rmsnorm(a[:, :_MLA_Q_RANK], gq_ref[...]).astype(_BF16)
    ckv = _rmsnorm(a[:, _MLA_Q_RANK:kv_lo], gkv_ref[...]).astype(_BF16)
    k_rope = _rope_lanes(a[:, kv_lo:kv_lo + _LANES], c, su, sd, half)
    qf = jnp.dot(cq, wq_ref[...], preferred_element_type=_F32)
    kvf = jnp.dot(ckv, wkv_ref[...], preferred_element_type=_F32)
    for j in range(_MLA_HEADS):
        lo, hi = j * _LANES, (j + 1) * _LANES
        q_ref[:, lo:hi] = (_rope_lanes(qf[:, lo:hi], c, su, sd, half) * scale).astype(_BF16)
        k_ref[:, lo:hi] = (kvf[:, lo:hi] + k_rope).astype(_BF16)
    v_ref[...] = kvf[:, _MLA_HEADS * _LANES:].astype(_BF16)


def _mla_weights(w_a, w_q_b, w_kv_b):
    d = w_a.shape[0]
    kv_lo = _MLA_Q_RANK + _MLA_KV_RANK
    z = lambda *s: jnp.zeros(s, _F32)
    pad = _LANES - _MLA_NOPE - _MLA_ROPE
    wa = jnp.concatenate([w_a[:, :kv_lo], z(d, _MLA_NOPE), w_a[:, kv_lo:], z(d, pad)], axis=1)
    wq = w_q_b.reshape(_MLA_Q_RANK, _MLA_HEADS, _MLA_NOPE + _MLA_ROPE)
    wq = jnp.concatenate([wq, z(_MLA_Q_RANK, _MLA_HEADS, pad)], axis=2)
    wkv = w_kv_b.reshape(_MLA_KV_RANK, _MLA_HEADS, _MLA_NOPE + _MLA_V)
    wk = jnp.concatenate([wkv[:, :, :_MLA_NOPE], z(_MLA_KV_RANK, _MLA_HEADS, _LANES - _MLA_NOPE)], axis=2)
    wv = wkv[:, :, _MLA_NOPE:]
    wkv = jnp.concatenate([wk.reshape(_MLA_KV_RANK, -1), wv.reshape(_MLA_KV_RANK, -1)], axis=1)
    return wa.astype(_BF16), wq.reshape(_MLA_Q_RANK, -1).astype(_BF16), wkv.astype(_BF16)


def _mla_proj(x, g, w_a, gq, w_q_b, gkv, w_kv_b, seq):
    t, d = x.shape
    tm = _tile(seq, 512)
    nseq = seq // tm
    c, su, sd = _rope_tables(seq, _LANES, _MLA_NOPE, _MLA_ROPE)
    wa, wq, wkv = _mla_weights(w_a, w_q_b, w_kv_b)
    hq = _MLA_HEADS * _LANES
    hv = _MLA_HEADS * _MLA_V
    tab = pl.BlockSpec((tm, _LANES), lambda i: (i % nseq, 0))
    full = lambda a: pl.BlockSpec(a.shape, lambda i: (0, 0))
    row = lambda n: pl.BlockSpec((tm, n), lambda i: (i, 0))
    g, gq, gkv = g.reshape(1, -1), gq.reshape(1, -1), gkv.reshape(1, -1)
    scale = (_MLA_NOPE + _MLA_ROPE) ** -0.5
    return pl.pallas_call(
        functools.partial(_mla_proj_kernel, scale=scale),
        grid=(t // tm,),
        in_specs=[row(d), full(g), full(wa), full(gq), full(wq), full(gkv), full(wkv), tab, tab, tab],
        out_specs=[row(hq), row(hq), row(hv)],
        out_shape=[jax.ShapeDtypeStruct((t, hq), _BF16), jax.ShapeDtypeStruct((t, hq), _BF16),
                   jax.ShapeDtypeStruct((t, hv), _BF16)],
        compiler_params=_cparams("parallel"),
        name="mla_proj",
    )(x, g, wa, gq, wq, gkv, wkv, c, su, sd)


def _mla_attn_kernel(q_ref, ka_ref, kb_ref, v_ref, o_ref, *, tk):
    oa = _flash_rows(q_ref[:, :_LANES], ka_ref, v_ref, tk)
    ob = _flash_rows(q_ref[:, _LANES:], kb_ref, v_ref, tk)
    lane = lax.broadcasted_iota(_I32, oa.shape, 1)
    o_ref[...] = jnp.where(lane < _MLA_V, oa, ob).astype(_BF16)


def _mla_attn(q, k, v, batch, seq):
    t = q.shape[0]
    tq = _tile(seq, 256)
    tk = _tile(seq, 512)
    nq = seq // tq
    return pl.pallas_call(
        functools.partial(_mla_attn_kernel, tk=tk),
        grid=(batch, _MLA_HEADS // 2, nq),
        in_specs=[pl.BlockSpec((tq, 2 * _LANES), lambda b, h, i: (b * nq + i, h)),
                  pl.BlockSpec((seq, _LANES), lambda b, h, i: (b, 2 * h)),
                  pl.BlockSpec((seq, _LANES), lambda b, h, i: (b, 2 * h + 1)),
                  pl.BlockSpec((seq, _LANES), lambda b, h, i: (b, h))],
        out_specs=pl.BlockSpec((tq, _LANES), lambda b, h, i: (b * nq + i, h)),
        out_shape=jax.ShapeDtypeStruct((t, _MLA_HEADS * _MLA_V), _BF16),
        compiler_params=_cparams("parallel", "parallel", "arbitrary"),
        name="mla_attn",
    )(q, k, k, v)


def _oproj_router_kernel(o_ref, w_ref, x_ref, g_ref, rwh_ref, rwl_ref, x1_ref, hn_ref, aff_ref):
    x1 = x_ref[...] + jnp.dot(o_ref[...], w_ref[...], preferred_element_type=_F32)
    x1_ref[...] = x1
    hn = _rmsnorm(x1, g_ref[...])
    hn_ref[...] = hn
    hh = hn.astype(_BF16)
    hl = (hn - hh.astype(_F32)).astype(_BF16)
    nt = (((1,), (1,)), ((), ()))
    rwh = rwh_ref[...]
    logits = (lax.dot_general(rwh, hh, nt, preferred_element_type=_F32)
              + lax.dot_general(rwh, hl, nt, preferred_element_type=_F32)
              + lax.dot_general(rwl_ref[...], hh, nt, preferred_element_type=_F32))
    e = jnp.exp(logits - jnp.max(logits, axis=0, keepdims=True))
    aff_ref[...] = e / jnp.sum(e, axis=0, keepdims=True)


def _oproj_router(o, w_o, x, g, router_w):
    t, d = x.shape
    tm = _tile(t, 512)
    ne = router_w.shape[1]
    rwt = router_w.T
    rwh = rwt.astype(_BF16)
    rwl = (rwt - rwh.astype(_F32)).astype(_BF16)
    row = pl.BlockSpec((tm, d), lambda i: (i, 0))
    full = lambda a: pl.BlockSpec(a.shape, lambda i: (0, 0))
    g = g.reshape(1, d)
    w = w_o.astype(_BF16)
    return pl.pallas_call(
        _oproj_router_kernel,
        grid=(t // tm,),
        in_specs=[row, full(w), row, full(g), full(rwh), full(rwl)],
        out_specs=[row, row, pl.BlockSpec((ne, tm), lambda i: (0, i))],
        out_shape=[jax.ShapeDtypeStruct((t, d), _F32), jax.ShapeDtypeStruct((t, d), _F32),
                   jax.ShapeDtypeStruct((ne, t), _F32)],
        compiler_params=_cparams("parallel"),
        name="oproj_router",
    )(o, w, x, g, rwh, rwl)


def _select_kernel(aff_ref, mask_ref, *, cap, n_experts):
    rows = aff_ref.shape[0] // n_experts
    lt = (lax.broadcasted_iota(_I32, (_LANES, _LANES), 0)
          <= lax.broadcasted_iota(_I32, (_LANES, _LANES), 1)).astype(_BF16)
    below = (lax.broadcasted_iota(_I32, (rows, rows), 1)
             < lax.broadcasted_iota(_I32, (rows, rows), 0)).astype(_BF16)
    capf = jnp.float32(cap)
    for e in range(n_experts):
        bits = pltpu.bitcast(aff_ref[e * rows:(e + 1) * rows, :], _I32)

        def count_ge(cand):
            return jnp.sum(jnp.where(bits >= cand, 1.0, 0.0), keepdims=True)

        def step(i, thr):
            cand = thr | lax.shift_left(jnp.int32(1), 30 - i)
            return jnp.where(count_ge(cand) >= capf, cand, thr)

        thr = lax.fori_loop(0, 31, step, jnp.zeros((1, 1), _I32))
        gt = bits > thr
        eq = jnp.where(bits == thr, 1.0, 0.0)
        need = capf - jnp.sum(jnp.where(gt, 1.0, 0.0), keepdims=True)
        incl = jnp.dot(eq.astype(_BF16), lt, preferred_element_type=_F32)
        tot = jnp.broadcast_to(jnp.sum(eq, axis=-1, keepdims=True), eq.shape).astype(_BF16)
        base = jnp.dot(below, tot, preferred_element_type=_F32)
        rank = base + incl - eq
        take = (eq > 0.0) & (rank < need)
        mask_ref[e * rows:(e + 1) * rows, :] = jnp.where(gt | take, 1, 0).astype(_I32)


def _select(aff_t, cap):
    ne, t = aff_t.shape
    rows = t // _LANES
    aff2 = aff_t.reshape(ne * rows, _LANES)
    mask = pl.pallas_call(
        functools.partial(_select_kernel, cap=cap, n_experts=ne),
        out_shape=jax.ShapeDtypeStruct(aff2.shape, _I32),
        compiler_params=pltpu.CompilerParams(vmem_limit_bytes=_VMEM_LIMIT),
        name="expert_select",
    )(aff2)
    return mask.reshape(ne, t)


def _ffn_kernel(idx_ref, hn_hbm, gate_ref, wg_ref, wu_ref, wd_ref, ye_ref, xbuf, sem, *, f_chunk):
    tc = xbuf.shape[0]

    def row_copy(j, t):
        return pltpu.make_async_copy(hn_hbm.at[pl.ds(t, 1)], xbuf.at[pl.ds(j, 1)], sem)

    def issue(j, carry):
        row_copy(j, idx_ref[0, 0, j]).start()
        return carry

    def drain(j, carry):
        row_copy(j, 0).wait()
        return carry

    lax.fori_loop(0, tc, issue, 0)
    lax.fori_loop(0, tc, drain, 0)
    x = xbuf[...].astype(_BF16)
    d_ff = wg_ref.shape[-1]
    y = jnp.zeros((tc, wd_ref.shape[-1]), _F32)
    for f in range(0, d_ff, f_chunk):
        g = jnp.dot(x, wg_ref[0, :, f:f + f_chunk], preferred_element_type=_F32)
        u = jnp.dot(x, wu_ref[0, :, f:f + f_chunk], preferred_element_type=_F32)
        hid = (g * jax.nn.sigmoid(g) * u).astype(_BF16)
        y = y + jnp.dot(hid, wd_ref[0, f:f + f_chunk, :], preferred_element_type=_F32)
    ye_ref[0] = y * gate_ref[0]


def _expert_ffn(hn, idx, gates, wg, wu, wd):
    t, d = hn.shape
    ne, cap = idx.shape
    d_ff = wg.shape[-1]
    tc = _tile(cap, 512)
    nc = cap // tc
    return pl.pallas_call(
        functools.partial(_ffn_kernel, f_chunk=_tile(d_ff, 512)),
        grid=(ne, nc),
        in_specs=[pl.BlockSpec((1, 1, tc), lambda e, j: (e * nc + j, 0, 0), memory_space=pltpu.SMEM),
                  pl.BlockSpec(memory_space=pl.ANY),
                  pl.BlockSpec((1, tc, 1), lambda e, j: (e, j, 0)),
                  pl.BlockSpec((1, d, d_ff), lambda e, j: (e, 0, 0)),
                  pl.BlockSpec((1, d, d_ff), lambda e, j: (e, 0, 0)),
                  pl.BlockSpec((1, d_ff, d), lambda e, j: (e, 0, 0))],
        out_specs=pl.BlockSpec((1, tc, d), lambda e, j: (e, j, 0)),
        out_shape=jax.ShapeDtypeStruct((ne, cap, d), _F32),
        scratch_shapes=[pltpu.VMEM((tc, d), _F32), pltpu.SemaphoreType.DMA],
        compiler_params=_cparams("arbitrary", "arbitrary"),
        name="expert_ffn",
    )(idx.reshape(ne * nc, 1, tc), hn, gates.reshape(ne, cap, 1), wg, wu, wd)


def _final_norm_kernel(x_ref, g_ref, o_ref):
    o_ref[...] = _rmsnorm(x_ref[...], g_ref[...])


def _final_norm(x, g):
    t, d = x.shape
    tm = _tile(t, 1024)
    row = pl.BlockSpec((tm, d), lambda i: (i, 0))
    return pl.pallas_call(
        _final_norm_kernel,
        grid=(t // tm,),
        in_specs=[row, pl.BlockSpec((1, d), lambda i: (0, 0))],
        out_specs=row,
        out_shape=jax.ShapeDtypeStruct((t, d), _F32),
        compiler_params=_cparams("parallel"),
        name="final_norm",
    )(x, g.reshape(1, d))


def _moe(x1, hn, aff_t, wg, wu, wd):
    t, d = x1.shape
    ne = aff_t.shape[0]
    cap = max(1, _CAPACITY_FACTOR * t // ne)
    mask = _select(aff_t, cap)
    idx = jax.vmap(lambda m: jnp.nonzero(m, size=cap)[0])(mask).astype(_I32)
    gates = jnp.take_along_axis(aff_t, idx, axis=1)
    ye = _expert_ffn(hn, idx, gates, wg, wu, wd)
    return x1.at[idx.reshape(-1)].add(ye.reshape(-1, d))


def _trunk(x, p):
    batch, seq, d = x.shape
    x = x.reshape(batch * seq, d)
    depth = p["ffn_norm_g"].shape[0]
    for i in range(depth):
        j = i // _N_MIXERS
        if i % _N_MIXERS == 0:
            lambda_init = 0.8 - 0.6 * float(np.exp(-0.3 * i))
            q, k, v = _diff_qkv(x, p["diff_norm_g"][j], p["diff_w_qkv"][j], seq)
            o = _diff_attn(q, k, v, p["diff_lambda_q1"][j], p["diff_lambda_k1"][j],
                           p["diff_lambda_q2"][j], p["diff_lambda_k2"][j], p["diff_subln_g"][j],
                           batch, seq, lambda_init)
            w_o = p["diff_w_o"][j]
        else:
            q, k, v = _mla_proj(x, p["mla_norm_g"][j], p["mla_w_a"][j], p["mla_q_norm_g"][j],
                                p["mla_w_q_b"][j], p["mla_kv_norm_g"][j], p["mla_w_kv_b"][j], seq)
            o = _mla_attn(q, k, v, batch, seq)
            w_o = p["mla_w_o"][j]
        x1, hn, aff_t = _oproj_router(o, w_o, x, p["ffn_norm_g"][i], p["router_w"][i])
        x = _moe(x1, hn, aff_t, p["w_gate_bf16"][i], p["w_up_bf16"][i], p["w_down_bf16"][i])
    return _final_norm(x, p["final_norm_g"]).reshape(batch, seq, d)


def kernel(x_prompt, x_sample, diff_norm_g, diff_w_qkv, diff_lambda_q1, diff_lambda_k1, diff_lambda_q2,
           diff_lambda_k2, diff_subln_g, diff_w_o, mla_norm_g, mla_w_a, mla_q_norm_g, mla_w_q_b,
           mla_kv_norm_g, mla_w_kv_b, mla_w_o, ffn_norm_g, router_w, w_gate, w_up, w_down, final_norm_g):
    p = dict(diff_norm_g=diff_norm_g, diff_w_qkv=diff_w_qkv, diff_lambda_q1=diff_lambda_q1,
             diff_lambda_k1=diff_lambda_k1, diff_lambda_q2=diff_lambda_q2, diff_lambda_k2=diff_lambda_k2,
             diff_subln_g=diff_subln_g, diff_w_o=diff_w_o, mla_norm_g=mla_norm_g, mla_w_a=mla_w_a,
             mla_q_norm_g=mla_q_norm_g, mla_w_q_b=mla_w_q_b, mla_kv_norm_g=mla_kv_norm_g,
             mla_w_kv_b=mla_w_kv_b, mla_w_o=mla_w_o, ffn_norm_g=ffn_norm_g, router_w=router_w,
             final_norm_g=final_norm_g, w_gate_bf16=w_gate.astype(_BF16), w_up_bf16=w_up.astype(_BF16),
             w_down_bf16=w_down.astype(_BF16))
    return _trunk(x_prompt, p), _trunk(x_sample, p)
```

```python
import functools

import numpy as np
import jax
import jax.numpy as jnp
from jax import lax
from jax.experimental import pallas as pl
from jax.experimental.pallas import tpu as pltpu

_F32, _BF16, _I32 = jnp.float32, jnp.bfloat16, jnp.int32
_EPS = 1e-6
_ROPE_THETA = 500000.0
_LANES = 128
_NEG = -1e30
_VMEM_LIMIT = 56 * 1024 * 1024

_DIFF_HEADS, _DIFF_HEAD_DIM, _DIFF_ROT = 8, 64, 16
_MLA_HEADS, _MLA_NOPE, _MLA_ROPE, _MLA_V = 16, 64, 32, 64
_MLA_Q_RANK, _MLA_KV_RANK = 384, 256
_N_EXPERTS, _CAPACITY_FACTOR = 16, 2
_N_MIXERS = 2


def _cparams(*sem):
    return pltpu.CompilerParams(dimension_semantics=sem, vmem_limit_bytes=_VMEM_LIMIT)


def _tile(n, pref):
    t = min(n, pref)
    assert n % t == 0, (n, pref)
    return t


def _rmsnorm(x, g):
    return x * lax.rsqrt(jnp.mean(x * x, axis=-1, keepdims=True) + _EPS) * g


def _rope_lanes(y, c, s_up, s_dn, half):
    return y * c + pltpu.roll(y, _LANES - half, 1) * s_up + pltpu.roll(y, half, 1) * s_dn


def _rope_tables(seq, group, start, rot):
    half = rot // 2
    pos = jnp.arange(seq, dtype=_F32)
    inv = jnp.float32(_ROPE_THETA) ** (-jnp.arange(0, rot, 2, dtype=_F32) / rot)
    ang = pos[:, None] * inv[None, :]
    cos, sin = jnp.cos(ang), jnp.sin(ang)
    j = (np.arange(_LANES) % group) - start
    first = (j >= 0) & (j < half)
    second = (j >= half) & (j < rot)
    f = np.where(first, j, np.where(second, j - half, 0))
    c = jnp.where(first | second, cos[:, f], 1.0)
    s_up = jnp.where(first, -sin[:, f], 0.0)
    s_dn = jnp.where(second, sin[:, f], 0.0)
    return c, s_up, s_dn


def _flash_scratch(n, rows, tk, dv):
    return [pltpu.VMEM((n, rows, tk), _F32), pltpu.VMEM((n, rows, tk), _BF16),
            pltpu.VMEM((n, rows, dv), _F32)]


def _flash_multi(qs, k_refs, v_ref, s_scr, p_scr, acc_scr):
    n, rows, tk = s_scr.shape
    n_chunks = v_ref.shape[0] // tk

    def chunk(ref, j):
        return ref[pl.ds(pl.multiple_of(j * tk, tk), tk), :]

    def scores(j):
        out = []
        for c in range(n):
            s = lax.dot_general(qs[c], chunk(k_refs[c], j), (((1,), (1,)), ((), ())),
                                preferred_element_type=_F32)
            s_scr[c] = s
            out.append(jnp.max(s, axis=-1, keepdims=True))
        return out

    def softmax(smax, st):
        out = []
        for c in range(n):
            m, l, _ = st[c]
            m_new = jnp.maximum(m, smax[c])
            alpha = jnp.exp(m - m_new)
            p = jnp.exp(s_scr[c] - m_new)
            p_scr[c] = p.astype(_BF16)
            out.append((m_new, alpha * l + jnp.sum(p, axis=-1, keepdims=True), alpha))
        return out

    def accumulate(st, j):
        for c in range(n):
            acc_scr[c] = st[c][2] * acc_scr[c] + jnp.dot(p_scr[c], chunk(v_ref, j),
                                                         preferred_element_type=_F32)

    acc_scr[...] = jnp.zeros(acc_scr.shape, _F32)
    st = [(jnp.full((rows, 1), _NEG, _F32), jnp.zeros((rows, 1), _F32), None)] * n
    st = softmax(scores(0), st)
    if n_chunks > 1:
        def body(j, carry):
            smax, st = carry
            accumulate(st, j - 1)
            st = softmax(smax, st)
            return scores(j + 1), st

        smax, st = lax.fori_loop(1, n_chunks - 1, body, (scores(1), st))
        accumulate(st, n_chunks - 2)
        st = softmax(smax, st)
    accumulate(st, n_chunks - 1)
    return [acc_scr[c] * (1.0 / st[c][1]) for c in range(n)]


def _diff_qkv_kernel(x_ref, g_ref, w_ref, c_ref, su_ref, sd_ref, q_ref, k_ref, v_ref, *, scale):
    h = _rmsnorm(x_ref[...], g_ref[...]).astype(_BF16)
    y = jnp.dot(h, w_ref[...], preferred_element_type=_F32)
    d = q_ref.shape[-1]
    c, su, sd = c_ref[...], su_ref[...], sd_ref[...]
    half = _DIFF_ROT // 2
    for j in range(d // _LANES):
        lo, hi = j * _LANES, (j + 1) * _LANES
        q_ref[:, lo:hi] = (_rope_lanes(y[:, lo:hi], c, su, sd, half) * scale).astype(_BF16)
        k_ref[:, lo:hi] = _rope_lanes(y[:, d + lo:d + hi], c, su, sd, half).astype(_BF16)
    v_ref[...] = y[:, 2 * d:].astype(_BF16)


def _diff_qkv(x, g, w, seq):
    t, d = x.shape
    tm = _tile(seq, 512)
    c, su, sd = _rope_tables(seq, _DIFF_HEAD_DIM, 0, _DIFF_ROT)
    nseq = seq // tm
    tab = pl.BlockSpec((tm, _LANES), lambda i: (i % nseq, 0))
    row = pl.BlockSpec((tm, d), lambda i: (i, 0))
    out = jax.ShapeDtypeStruct((t, d), _BF16)
    return pl.pallas_call(
        functools.partial(_diff_qkv_kernel, scale=_DIFF_HEAD_DIM ** -0.5),
        grid=(t // tm,),
        in_specs=[row, pl.BlockSpec((1, d), lambda i: (0, 0)),
                  pl.BlockSpec((d, 3 * d), lambda i: (0, 0)), tab, tab, tab],
        out_specs=[row, row, row],
        out_shape=[out, out, out],
        compiler_params=_cparams("parallel"),
        name="diff_qkv",
    )(x, g.reshape(1, d), w.astype(_BF16), c, su, sd)


def _diff_attn_kernel(q_ref, k_ref, v_ref, lq1_ref, lk1_ref, lq2_ref, lk2_ref, g_ref, o_ref,
                      s_scr, p_scr, acc_scr, *, lambda_init):
    q = q_ref[...]
    lane = lax.broadcasted_iota(_I32, q.shape, 1)
    zero = jnp.zeros_like(q)
    q0 = jnp.where(lane < _DIFF_HEAD_DIM, q, zero)
    q1 = jnp.where(lane >= _DIFF_HEAD_DIM, q, zero)
    o0, o1 = _flash_multi([q0, q1], [k_ref, k_ref], v_ref, s_scr, p_scr, acc_scr)
    lam = (jnp.exp(jnp.sum(lq1_ref[...] * lk1_ref[...], axis=-1, keepdims=True))
           - jnp.exp(jnp.sum(lq2_ref[...] * lk2_ref[...], axis=-1, keepdims=True)) + lambda_init)
    o = o0 - lam * o1
    o = _rmsnorm(o, g_ref[...]) * (1.0 - lambda_init)
    o_ref[...] = o.astype(_BF16)


def _diff_attn(q, k, v, lq1, lk1, lq2, lk2, subln_g, batch, seq, lambda_init):
    t, d = q.shape
    tq = _tile(seq, 256)
    tk = _tile(seq, 512)
    nq = seq // tq
    qspec = pl.BlockSpec((tq, _LANES), lambda b, h, i: (b * nq + i, h))
    kvspec = pl.BlockSpec((seq, _LANES), lambda b, h, i: (b, h))
    small = lambda n: pl.BlockSpec((1, n), lambda b, h, i: (0, 0))
    hd = _DIFF_HEAD_DIM
    return pl.pallas_call(
        functools.partial(_diff_attn_kernel, lambda_init=lambda_init),
        grid=(batch, _DIFF_HEADS, nq),
        in_specs=[qspec, kvspec, kvspec, small(hd), small(hd), small(hd), small(hd), small(2 * hd)],
        out_specs=qspec,
        out_shape=jax.ShapeDtypeStruct((t, d), _BF16),
        scratch_shapes=_flash_scratch(2, tq, tk, _LANES),
        compiler_params=_cparams("parallel", "parallel", "arbitrary"),
        name="diff_attn",
    )(q, k, v, lq1.reshape(1, hd), lk1.reshape(1, hd), lq2.reshape(1, hd), lk2.reshape(1, hd),
      subln_g.reshape(1, 2 * hd))


def _mla_proj_kernel(x_ref, g_ref, wa_ref, gq_ref, wq_ref, gkv_ref, wkv_ref, c_ref, su_ref, sd_ref,
                     q_ref, k_ref, v_ref, *, scale):
    h = _rmsnorm(x_ref[...], g_ref[...]).astype(_BF16)
    a = jnp.dot(h, wa_ref[...], preferred_element_type=_F32)
    c, su, sd = c_ref[...], su_ref[...], sd_ref[...]
    half = _MLA_ROPE // 2
    kv_lo = _MLA_Q_RANK + _MLA_KV_RANK
    cq = _rmsnorm(a[:, :_MLA_Q_RANK], gq_ref[...]).astype(_BF16)
    ckv = _rmsnorm(a[:, _MLA_Q_RANK:kv_lo], gkv_ref[...]).astype(_BF16)
    k_rope = _rope_lanes(a[:, kv_lo:kv_lo + _LANES], c, su, sd, half)
    qf = jnp.dot(cq, wq_ref[...], preferred_element_type=_F32)
    kvf = jnp.dot(ckv, wkv_ref[...], preferred_element_type=_F32)
    for j in range(_MLA_HEADS):
        lo, hi = j * _LANES, (j + 1) * _LANES
        q_ref[:, lo:hi] = (_rope_lanes(qf[:, lo:hi], c, su, sd, half) * scale).astype(_BF16)
        k_ref[:, lo:hi] = (kvf[:, lo:hi] + k_rope).astype(_BF16)
    v_ref[...] = kvf[:, _MLA_HEADS * _LANES:].astype(_BF16)


def _mla_weights(w_a, w_q_b, w_kv_b):
    d = w_a.shape[0]
    kv_lo = _MLA_Q_RANK + _MLA_KV_RANK
    z = lambda *s: jnp.zeros(s, _F32)
    pad = _LANES - _MLA_NOPE - _MLA_ROPE
    wa = jnp.concatenate([w_a[:, :kv_lo], z(d, _MLA_NOPE), w_a[:, kv_lo:], z(d, pad)], axis=1)
    wq = w_q_b.reshape(_MLA_Q_RANK, _MLA_HEADS, _MLA_NOPE + _MLA_ROPE)
    wq = jnp.concatenate([wq, z(_MLA_Q_RANK, _MLA_HEADS, pad)], axis=2)
    wkv = w_kv_b.reshape(_MLA_KV_RANK, _MLA_HEADS, _MLA_NOPE + _MLA_V)
    wk = jnp.concatenate([wkv[:, :, :_MLA_NOPE], z(_MLA_KV_RANK, _MLA_HEADS, _LANES - _MLA_NOPE)], axis=2)
    wv = wkv[:, :, _MLA_NOPE:]
    wkv = jnp.concatenate([wk.reshape(_MLA_KV_RANK, -1), wv.reshape(_MLA_KV_RANK, -1)], axis=1)
    return wa.astype(_BF16), wq.reshape(_MLA_Q_RANK, -1).astype(_BF16), wkv.astype(_BF16)


def _mla_proj(x, g, w_a, gq, w_q_b, gkv, w_kv_b, seq):
    t, d = x.shape
    tm = _tile(seq, 512)
    nseq = seq // tm
    c, su, sd = _rope_tables(seq, _LANES, _MLA_NOPE, _MLA_ROPE)
    wa, wq, wkv = _mla_weights(w_a, w_q_b, w_kv_b)
    hq = _MLA_HEADS * _LANES
    hv = _MLA_HEADS * _MLA_V
    tab = pl.BlockSpec((tm, _LANES), lambda i: (i % nseq, 0))
    full = lambda a: pl.BlockSpec(a.shape, lambda i: (0, 0))
    row = lambda n: pl.BlockSpec((tm, n), lambda i: (i, 0))
    g, gq, gkv = g.reshape(1, -1), gq.reshape(1, -1), gkv.reshape(1, -1)
    scale = (_MLA_NOPE + _MLA_ROPE) ** -0.5
    return pl.pallas_call(
        functools.partial(_mla_proj_kernel, scale=scale),
        grid=(t // tm,),
        in_specs=[row(d), full(g), full(wa), full(gq), full(wq), full(gkv), full(wkv), tab, tab, tab],
        out_specs=[row(hq), row(hq), row(hv)],
        out_shape=[jax.ShapeDtypeStruct((t, hq), _BF16), jax.ShapeDtypeStruct((t, hq), _BF16),
                   jax.ShapeDtypeStruct((t, hv), _BF16)],
        compiler_params=_cparams("parallel"),
        name="mla_proj",
    )(x, g, wa, gq, wq, gkv, wkv, c, su, sd)


def _mla_attn_kernel(q_ref, ka_ref, kb_ref, v_ref, o_ref, s_scr, p_scr, acc_scr):
    oa, ob = _flash_multi([q_ref[:, :_LANES], q_ref[:, _LANES:]], [ka_ref, kb_ref], v_ref,
                          s_scr, p_scr, acc_scr)
    lane = lax.broadcasted_iota(_I32, oa.shape, 1)
    o_ref[...] = jnp.where(lane < _MLA_V, oa, ob).astype(_BF16)


def _mla_attn(q, k, v, batch, seq):
    t = q.shape[0]
    tq = _tile(seq, 256)
    tk = _tile(seq, 512)
    nq = seq // tq
    return pl.pallas_call(
        _mla_attn_kernel,
        grid=(batch, _MLA_HEADS // 2, nq),
        in_specs=[pl.BlockSpec((tq, 2 * _LANES), lambda b, h, i: (b * nq + i, h)),
                  pl.BlockSpec((seq, _LANES), lambda b, h, i: (b, 2 * h)),
                  pl.BlockSpec((seq, _LANES), lambda b, h, i: (b, 2 * h + 1)),
                  pl.BlockSpec((seq, _LANES), lambda b, h, i: (b, h))],
        out_specs=pl.BlockSpec((tq, _LANES), lambda b, h, i: (b * nq + i, h)),
        out_shape=jax.ShapeDtypeStruct((t, _MLA_HEADS * _MLA_V), _BF16),
        scratch_shapes=_flash_scratch(2, tq, tk, _LANES),
        compiler_params=_cparams("parallel", "parallel", "arbitrary"),
        name="mla_attn",
    )(q, k, k, v)


def _oproj_router_kernel(o_ref, w_ref, x_ref, g_ref, rwh_ref, rwl_ref, x1_ref, hn_ref, aff_ref):
    x1 = x_ref[...] + jnp.dot(o_ref[...], w_ref[...], preferred_element_type=_F32)
    x1_ref[...] = x1
    hn = _rmsnorm(x1, g_ref[...])
    hn_ref[...] = hn
    hh = hn.astype(_BF16)
    hl = (hn - hh.astype(_F32)).astype(_BF16)
    nt = (((1,), (1,)), ((), ()))
    rwh = rwh_ref[...]
    logits = (lax.dot_general(rwh, hh, nt, preferred_element_type=_F32)
              + lax.dot_general(rwh, hl, nt, preferred_element_type=_F32)
              + lax.dot_general(rwl_ref[...], hh, nt, preferred_element_type=_F32))
    e = jnp.exp(logits - jnp.max(logits, axis=0, keepdims=True))
    aff_ref[...] = e / jnp.sum(e, axis=0, keepdims=True)


def _oproj_router(o, w_o, x, g, router_w):
    t, d = x.shape
    tm = _tile(t, 512)
    ne = router_w.shape[1]
    rwt = router_w.T
    rwh = rwt.astype(_BF16)
    rwl = (rwt - rwh.astype(_F32)).astype(_BF16)
    row = pl.BlockSpec((tm, d), lambda i: (i, 0))
    full = lambda a: pl.BlockSpec(a.shape, lambda i: (0, 0))
    g = g.reshape(1, d)
    w = w_o.astype(_BF16)
    return pl.pallas_call(
        _oproj_router_kernel,
        grid=(t // tm,),
        in_specs=[row, full(w), row, full(g), full(rwh), full(rwl)],
        out_specs=[row, row, pl.BlockSpec((ne, tm), lambda i: (0, i))],
        out_shape=[jax.ShapeDtypeStruct((t, d), _F32), jax.ShapeDtypeStruct((t, d), _F32),
                   jax.ShapeDtypeStruct((ne, t), _F32)],
        compiler_params=_cparams("parallel"),
        name="oproj_router",
    )(o, w, x, g, rwh, rwl)


def _select_kernel(aff_ref, idx_ref, gate_ref, base_ref, *, cap, n_experts, ns):
    rows = aff_ref.shape[0] // n_experts

    def iota(shape, dim):
        return lax.broadcasted_iota(_I32, shape, dim)

    sq = (_LANES, _LANES)
    upto = (iota(sq, 0) <= iota(sq, 1)).astype(_BF16)
    upto_t = (iota(sq, 1) <= iota(sq, 0)).astype(_BF16)
    above = (iota((rows, rows), 1) < iota((rows, rows), 0)).astype(_BF16)
    capf = jnp.float32(cap)
    row_id = iota((rows, ns), 0).astype(_F32)
    lane_id = iota((_LANES, ns), 0).astype(_F32)
    slot_id = iota((1, ns), 1).astype(_F32)

    def lanes(col):
        return jnp.broadcast_to(col, (rows, _LANES))

    def rows_before(tot):
        return jnp.dot(above, lanes(tot).astype(_BF16), preferred_element_type=_F32)[:, :1]

    def expert(e, carry):
        r0 = pl.multiple_of(e * rows, rows)
        aff = aff_ref[pl.ds(r0, rows), :]
        bits = pltpu.bitcast(aff, _I32)

        def step(i, thr):
            cand = thr | lax.shift_left(jnp.int32(1), jnp.int32(30) - lax.convert_element_type(i, _I32))
            return jnp.where(jnp.sum(jnp.where(bits >= cand, 1.0, 0.0), keepdims=True) >= capf, cand, thr)

        thr = lax.fori_loop(0, 31, step, jnp.zeros((1, 1), _I32))
        gt = bits > thr
        eq = jnp.where(bits == thr, 1.0, 0.0)
        need = capf - jnp.sum(jnp.where(gt, 1.0, 0.0), keepdims=True)
        rank = (rows_before(jnp.sum(eq, axis=-1, keepdims=True))
                + jnp.dot(eq.astype(_BF16), upto, preferred_element_type=_F32) - eq)
        sel = jnp.where(gt | ((eq > 0.0) & (rank < need)), 1.0, 0.0)

        tot = jnp.sum(sel, axis=-1, keepdims=True)
        base = rows_before(tot)
        cum = base + tot
        base_ref[pl.ds(r0, rows), :] = lanes(base).astype(_I32)
        incl_t = jnp.dot(upto_t, sel.T.astype(_BF16), preferred_element_type=_F32).astype(_BF16)
        aff_t = aff.T
        a0 = aff_t.astype(_BF16)
        a1 = (aff_t - a0.astype(_F32)).astype(_BF16)
        a2 = (aff_t - a0.astype(_F32) - a1.astype(_F32)).astype(_BF16)
        for c in range(cap // ns):
            j = slot_id + float(c * ns)
            rj = jnp.sum(jnp.where(cum <= j, 1.0, 0.0), axis=0, keepdims=True)
            hit = row_id == rj
            q = j - jnp.sum(jnp.where(hit, base, 0.0), axis=0, keepdims=True)
            onehot = jnp.where(hit, 1.0, 0.0).astype(_BF16)
            counts = jnp.dot(incl_t, onehot, preferred_element_type=_F32)
            lpos = jnp.sum(jnp.where(counts <= q, 1.0, 0.0), axis=0, keepdims=True)
            arow = (jnp.dot(a0, onehot, preferred_element_type=_F32)
                    + jnp.dot(a1, onehot, preferred_element_type=_F32)
                    + jnp.dot(a2, onehot, preferred_element_type=_F32))
            gate = jnp.sum(jnp.where(lane_id == lpos, arow, 0.0), axis=0, keepdims=True)
            idx_ref[e, pl.ds(c, 1), :] = (rj * float(_LANES) + lpos).astype(_I32)
            gate_ref[e, pl.ds(c, 1), :] = gate
        return carry

    lax.fori_loop(0, n_experts, expert, 0)


def _select(aff_t, cap):
    ne, t = aff_t.shape
    rows = t // _LANES
    ns = _tile(cap, 512)
    aff2 = aff_t.reshape(ne * rows, _LANES)
    listing = jax.ShapeDtypeStruct((ne, cap // ns, ns), _I32)
    idx, gates, base = pl.pallas_call(
        functools.partial(_select_kernel, cap=cap, n_experts=ne, ns=ns),
        out_shape=[listing, jax.ShapeDtypeStruct(listing.shape, _F32),
                   jax.ShapeDtypeStruct(aff2.shape, _I32)],
        compiler_params=pltpu.CompilerParams(vmem_limit_bytes=_VMEM_LIMIT),
        name="expert_select",
    )(aff2)
    return idx.reshape(ne, cap), gates.reshape(ne, cap), base.reshape(ne, rows, _LANES)[:, :, 0]


def _ffn_kernel(idx_ref, nxt_ref, hn_hbm, gate_ref, wg_ref, wu_ref, wd_ref, ye_ref, xbuf, sems,
                *, f_chunk):
    tc = xbuf.shape[1]
    n_tiles = pl.num_programs(1)
    step = pl.program_id(0) * n_tiles + pl.program_id(1)
    last = pl.num_programs(0) * n_tiles - 1
    slot = step % 2

    def row_copy(j, t, buf):
        return pltpu.make_async_copy(hn_hbm.at[pl.ds(t, 1)], xbuf.at[buf, pl.ds(j, 1)], sems.at[buf])

    def gather(ids_ref, buf):
        def issue(j, carry):
            row_copy(j, ids_ref[0, 0, j], buf).start()
            return carry
        lax.fori_loop(0, tc, issue, 0, unroll=8)

    @pl.when(step == 0)
    def _():
        gather(idx_ref, 0)

    @pl.when(step < last)
    def _():
        gather(nxt_ref, 1 - slot)

    def drain(j, carry):
        row_copy(j, 0, slot).wait()
        return carry

    lax.fori_loop(0, tc, drain, 0, unroll=True)
    x = xbuf[slot].astype(_BF16)
    d_ff = wg_ref.shape[-1]
    y = jnp.zeros((tc, wd_ref.shape[-1]), _F32)
    for f in range(0, d_ff, f_chunk):
        g = jnp.dot(x, wg_ref[0, :, f:f + f_chunk], preferred_element_type=_F32)
        u = jnp.dot(x, wu_ref[0, :, f:f + f_chunk], preferred_element_type=_F32)
        hid = (g * jax.nn.sigmoid(g) * u).astype(_BF16)
        y = y + jnp.dot(hid, wd_ref[0, f:f + f_chunk, :], preferred_element_type=_F32)
    ye_ref[0] = y * gate_ref[0]


def _expert_ffn(hn, idx, gates, wg, wu, wd):
    t, d = hn.shape
    ne, cap = idx.shape
    d_ff = wg.shape[-1]
    tc = _tile(cap, 512)
    nc = cap // tc
    last = ne * nc - 1
    ids = idx.reshape(ne * nc, 1, tc)
    return pl.pallas_call(
        functools.partial(_ffn_kernel, f_chunk=_tile(d_ff, 512)),
        grid=(ne, nc),
        in_specs=[pl.BlockSpec((1, 1, tc), lambda e, j: (e * nc + j, 0, 0), memory_space=pltpu.SMEM),
                  pl.BlockSpec((1, 1, tc), lambda e, j: (jnp.minimum(e * nc + j + 1, last), 0, 0),
                               memory_space=pltpu.SMEM),
                  pl.BlockSpec(memory_space=pl.ANY),
                  pl.BlockSpec((1, tc, 1), lambda e, j: (e, j, 0)),
                  pl.BlockSpec((1, d, d_ff), lambda e, j: (e, 0, 0)),
                  pl.BlockSpec((1, d, d_ff), lambda e, j: (e, 0, 0)),
                  pl.BlockSpec((1, d_ff, d), lambda e, j: (e, 0, 0))],
        out_specs=pl.BlockSpec((1, tc, d), lambda e, j: (e, j, 0)),
        out_shape=jax.ShapeDtypeStruct((ne, cap, d), _F32),
        scratch_shapes=[pltpu.VMEM((2, tc, d), _F32), pltpu.SemaphoreType.DMA((2,))],
        compiler_params=_cparams("arbitrary", "arbitrary"),
        name="expert_ffn",
    )(ids, ids, hn, gates.reshape(ne, cap, 1), wg, wu, wd)


_ROW_CHUNK = 8


def _div(x, n):
    if n & (n - 1) == 0:
        return lax.shift_right_logical(x, jnp.int32(n.bit_length() - 1))
    return lax.div(x, jnp.int32(n))


def _combine_kernel(lo_ref, x_ref, ye_hbm, *rest, n_experts, cap, idx_block):
    idx_refs = rest[:2 * n_experts]
    out_ref, stage, sems = rest[2 * n_experts:]
    tb = x_ref.shape[0]
    region = stage.shape[1] // n_experts
    b = pl.program_id(0)
    slot = b % 2

    def for_each_chunk(blk, buf, fn):
        for e in range(n_experts):
            lo, hi = lo_ref[e, blk], lo_ref[e, blk + 1]
            c0 = _div(lo, _ROW_CHUNK)

            def one(c, carry):
                src = pl.multiple_of(e * cap + c * _ROW_CHUNK, _ROW_CHUNK)
                dst = pl.multiple_of(e * region + (c - c0) * _ROW_CHUNK, _ROW_CHUNK)
                fn(pltpu.make_async_copy(ye_hbm.at[pl.ds(src, _ROW_CHUNK)],
                                         stage.at[buf, pl.ds(dst, _ROW_CHUNK)], sems.at[buf]))
                return carry

            lax.fori_loop(c0, _div(hi + (_ROW_CHUNK - 1), _ROW_CHUNK), one, 0)

    @pl.when(b == 0)
    def _():
        for_each_chunk(0, 0, lambda cp: cp.start())

    @pl.when(b + 1 < pl.num_programs(0))
    def _():
        for_each_chunk(b + 1, 1 - slot, lambda cp: cp.start())

    out_ref[...] = x_ref[...]
    for_each_chunk(b, slot, lambda cp: cp.wait())
    for e in range(n_experts):
        lo, hi = lo_ref[e, b], lo_ref[e, b + 1]
        first = _div(lo, idx_block) * idx_block
        row0 = e * region - _div(lo, _ROW_CHUNK) * _ROW_CHUNK
        ia, ib = idx_refs[2 * e], idx_refs[2 * e + 1]

        def one(j, carry):
            k = j - first
            kk = k & (idx_block - 1) if idx_block & (idx_block - 1) == 0 else lax.rem(k, idx_block)
            tok = jnp.where(k < idx_block, ia[0, 0, 0, kk], ib[0, 0, 0, kk]) - b * tb
            out_ref[pl.ds(tok, 1), :] = out_ref[pl.ds(tok, 1), :] + stage[slot, pl.ds(row0 + j, 1), :]
            return carry

        lax.fori_loop(lo, hi, one, 0)


def _combine(x1, ye, idx, before):
    t, d = x1.shape
    ne, cap = idx.shape
    tb = _tile(t, _LANES)
    nb = t // tb
    assert cap % _ROW_CHUNK == 0
    idx_block = min(tb, cap)
    n_ib = cap // idx_block
    lo = jnp.concatenate([before[:, ::tb // _LANES], jnp.full((ne, 1), cap, _I32)], axis=1)
    idx4 = idx.reshape(ne, n_ib, 1, idx_block)

    def idx_spec(e, w):
        return pl.BlockSpec(
            (1, 1, 1, idx_block),
            lambda b, lo_ref: (e, jnp.minimum(_div(lo_ref[e, b], idx_block) + w, n_ib - 1), 0, 0),
            memory_space=pltpu.SMEM)

    region = tb + _ROW_CHUNK
    row = pl.BlockSpec((tb, d), lambda b, lo_ref: (b, 0))
    grid_spec = pltpu.PrefetchScalarGridSpec(
        num_scalar_prefetch=1,
        grid=(nb,),
        in_specs=[row, pl.BlockSpec(memory_space=pl.ANY)]
                 + [idx_spec(e, w) for e in range(ne) for w in range(2)],
        out_specs=row,
        scratch_shapes=[pltpu.VMEM((2, ne * region, d), _F32), pltpu.SemaphoreType.DMA((2,))])
    return pl.pallas_call(
        functools.partial(_combine_kernel, n_experts=ne, cap=cap, idx_block=idx_block),
        grid_spec=grid_spec,
        out_shape=jax.ShapeDtypeStruct((t, d), _F32),
        compiler_params=_cparams("arbitrary"),
        name="moe_combine",
    )(lo, x1, ye.reshape(ne * cap, d), *([idx4] * (2 * ne)))


def _final_norm_kernel(x_ref, g_ref, o_ref):
    o_ref[...] = _rmsnorm(x_ref[...], g_ref[...])


def _final_norm(x, g):
    t, d = x.shape
    tm = _tile(t, 1024)
    row = pl.BlockSpec((tm, d), lambda i: (i, 0))
    return pl.pallas_call(
        _final_norm_kernel,
        grid=(t // tm,),
        in_specs=[row, pl.BlockSpec((1, d), lambda i: (0, 0))],
        out_specs=row,
        out_shape=jax.ShapeDtypeStruct((t, d), _F32),
        compiler_params=_cparams("parallel"),
        name="final_norm",
    )(x, g.reshape(1, d))


def _moe(x1, hn, aff_t, wg, wu, wd):
    t, d = x1.shape
    ne = aff_t.shape[0]
    cap = max(1, _CAPACITY_FACTOR * t // ne)
    idx, gates, before = _select(aff_t, cap)
    ye = _expert_ffn(hn, idx, gates, wg, wu, wd)
    return _combine(x1, ye, idx, before)


def _trunk(x, p):
    batch, seq, d = x.shape
    x = x.reshape(batch * seq, d)
    depth = p["ffn_norm_g"].shape[0]
    for i in range(depth):
        j = i // _N_MIXERS
        if i % _N_MIXERS == 0:
            lambda_init = 0.8 - 0.6 * float(np.exp(-0.3 * i))
            q, k, v = _diff_qkv(x, p["diff_norm_g"][j], p["diff_w_qkv"][j], seq)
            o = _diff_attn(q, k, v, p["diff_lambda_q1"][j], p["diff_lambda_k1"][j],
                           p["diff_lambda_q2"][j], p["diff_lambda_k2"][j], p["diff_subln_g"][j],
                           batch, seq, lambda_init)
            w_o = p["diff_w_o"][j]
        else:
            q, k, v = _mla_proj(x, p["mla_norm_g"][j], p["mla_w_a"][j], p["mla_q_norm_g"][j],
                                p["mla_w_q_b"][j], p["mla_kv_norm_g"][j], p["mla_w_kv_b"][j], seq)
            o = _mla_attn(q, k, v, batch, seq)
            w_o = p["mla_w_o"][j]
        x1, hn, aff_t = _oproj_router(o, w_o, x, p["ffn_norm_g"][i], p["router_w"][i])
        x = _moe(x1, hn, aff_t, p["w_gate_bf16"][i], p["w_up_bf16"][i], p["w_down_bf16"][i])
    return _final_norm(x, p["final_norm_g"]).reshape(batch, seq, d)


def kernel(x_prompt, x_sample, diff_norm_g, diff_w_qkv, diff_lambda_q1, diff_lambda_k1, diff_lambda_q2,
           diff_lambda_k2, diff_subln_g, diff_w_o, mla_norm_g, mla_w_a, mla_q_norm_g, mla_w_q_b,
           mla_kv_norm_g, mla_w_kv_b, mla_w_o, ffn_norm_g, router_w, w_gate, w_up, w_down, final_norm_g):
    p = dict(diff_norm_g=diff_norm_g, diff_w_qkv=diff_w_qkv, diff_lambda_q1=diff_lambda_q1,
             diff_lambda_k1=diff_lambda_k1, diff_lambda_q2=diff_lambda_q2, diff_lambda_k2=diff_lambda_k2,
             diff_subln_g=diff_subln_g, diff_w_o=diff_w_o, mla_norm_g=mla_norm_g, mla_w_a=mla_w_a,
             mla_q_norm_g=mla_q_norm_g, mla_w_q_b=mla_w_q_b, mla_kv_norm_g=mla_kv_norm_g,
             mla_w_kv_b=mla_w_kv_b, mla_w_o=mla_w_o, ffn_norm_g=ffn_norm_g, router_w=router_w,
             final_norm_g=final_norm_g, w_gate_bf16=w_gate.astype(_BF16), w_up_bf16=w_up.astype(_BF16),
             w_down_bf16=w_down.astype(_BF16))
    return _trunk(x_prompt, p), _trunk(x_sample, p)
```

```python
import functools

import numpy as np
import jax
import jax.numpy as jnp
from jax import lax
from jax.experimental import pallas as pl
from jax.experimental.pallas import tpu as pltpu

_F32, _BF16, _I32 = jnp.float32, jnp.bfloat16, jnp.int32
_EPS = 1e-6
_ROPE_THETA = 500000.0
_LANES = 128
_NEG = -1e30
_VMEM_LIMIT = 56 * 1024 * 1024

_DIFF_HEADS, _DIFF_HEAD_DIM, _DIFF_ROT = 8, 64, 16
_MLA_HEADS, _MLA_NOPE, _MLA_ROPE, _MLA_V = 16, 64, 32, 64
_MLA_Q_RANK, _MLA_KV_RANK = 384, 256
_N_EXPERTS, _CAPACITY_FACTOR = 16, 2
_N_MIXERS = 2


def _cparams(*sem):
    return pltpu.CompilerParams(dimension_semantics=sem, vmem_limit_bytes=_VMEM_LIMIT)


def _tile(n, pref):
    t = min(n, pref)
    assert n % t == 0, (n, pref)
    return t


def _rmsnorm(x, g):
    return x * lax.rsqrt(jnp.mean(x * x, axis=-1, keepdims=True) + _EPS) * g


def _rope_lanes(y, c, s_up, s_dn, half):
    return y * c + pltpu.roll(y, _LANES - half, 1) * s_up + pltpu.roll(y, half, 1) * s_dn


def _rope_tables(seq, group, start, rot):
    half = rot // 2
    pos = jnp.arange(seq, dtype=_F32)
    inv = jnp.float32(_ROPE_THETA) ** (-jnp.arange(0, rot, 2, dtype=_F32) / rot)
    ang = pos[:, None] * inv[None, :]
    cos, sin = jnp.cos(ang), jnp.sin(ang)
    j = (np.arange(_LANES) % group) - start
    first = (j >= 0) & (j < half)
    second = (j >= half) & (j < rot)
    f = np.where(first, j, np.where(second, j - half, 0))
    c = jnp.where(first | second, cos[:, f], 1.0)
    s_up = jnp.where(first, -sin[:, f], 0.0)
    s_dn = jnp.where(second, sin[:, f], 0.0)
    return c, s_up, s_dn


def _flash_scratch(n, tq, tk, dv):
    return [pltpu.VMEM((n, tk, tq), _F32), pltpu.VMEM((n, tk, tq), _BF16),
            pltpu.VMEM((n, dv, tq), _F32)]


def _flash_multi(qs, k_refs, vt_ref, s_scr, p_scr, acc_scr):
    n, tk, tq = s_scr.shape
    n_chunks = vt_ref.shape[0]

    def scores(j):
        out = []
        for c in range(n):
            ks = k_refs[c][pl.ds(pl.multiple_of(j * tk, tk), tk), :]
            s = lax.dot_general(ks, qs[c], (((1,), (1,)), ((), ())), preferred_element_type=_F32)
            s_scr[c] = s
            out.append(jnp.max(s, axis=0, keepdims=True))
        return out

    def softmax(smax, st):
        out = []
        for c in range(n):
            m, l, _ = st[c]
            m_new = jnp.maximum(m, smax[c])
            alpha = jnp.exp(m - m_new)
            p = jnp.exp(s_scr[c] - m_new)
            p_scr[c] = p.astype(_BF16)
            out.append((m_new, alpha * l + jnp.sum(p, axis=0, keepdims=True), alpha))
        return out

    def accumulate(st, j):
        for c in range(n):
            acc_scr[c] = st[c][2] * acc_scr[c] + jnp.dot(vt_ref[j], p_scr[c],
                                                         preferred_element_type=_F32)

    acc_scr[...] = jnp.zeros(acc_scr.shape, _F32)
    st = [(jnp.full((1, tq), _NEG, _F32), jnp.zeros((1, tq), _F32), None)] * n
    st = softmax(scores(0), st)
    if n_chunks > 1:
        def body(j, carry):
            smax, st = carry
            accumulate(st, j - 1)
            st = softmax(smax, st)
            return scores(j + 1), st

        smax, st = lax.fori_loop(1, n_chunks - 1, body, (scores(1), st))
        accumulate(st, n_chunks - 2)
        st = softmax(smax, st)
    accumulate(st, n_chunks - 1)
    return [(acc_scr[c] * (1.0 / st[c][1])).T for c in range(n)]


_NT = (((1,), (1,)), ((), ()))


def _store_vt(vt_ref, vt, heads):
    for j in range(heads):
        vt_ref[0, j, 0] = vt[j * _LANES:(j + 1) * _LANES, :].astype(_BF16)


def _diff_qkv_kernel(x_ref, g_ref, w_ref, wvt_ref, c_ref, su_ref, sd_ref, q_ref, k_ref, vt_ref,
                     *, scale):
    h = _rmsnorm(x_ref[...], g_ref[...]).astype(_BF16)
    y = jnp.dot(h, w_ref[...], preferred_element_type=_F32)
    d = q_ref.shape[-1]
    c, su, sd = c_ref[...], su_ref[...], sd_ref[...]
    half = _DIFF_ROT // 2
    for j in range(d // _LANES):
        lo, hi = j * _LANES, (j + 1) * _LANES
        q_ref[:, lo:hi] = (_rope_lanes(y[:, lo:hi], c, su, sd, half) * scale).astype(_BF16)
        k_ref[:, lo:hi] = _rope_lanes(y[:, d + lo:d + hi], c, su, sd, half).astype(_BF16)
    _store_vt(vt_ref, lax.dot_general(wvt_ref[...], h, _NT, preferred_element_type=_F32), _DIFF_HEADS)


def _vt_shape_spec(batch, heads, seq, tm):
    nseq = seq // tm
    return (jax.ShapeDtypeStruct((batch, heads, nseq, _LANES, tm), _BF16),
            pl.BlockSpec((1, heads, 1, _LANES, tm), lambda i: (i // nseq, 0, i % nseq, 0, 0)))


def _diff_qkv(x, g, w, batch, seq):
    t, d = x.shape
    tm = _tile(seq, 512)
    c, su, sd = _rope_tables(seq, _DIFF_HEAD_DIM, 0, _DIFF_ROT)
    nseq = seq // tm
    tab = pl.BlockSpec((tm, _LANES), lambda i: (i % nseq, 0))
    row = pl.BlockSpec((tm, d), lambda i: (i, 0))
    out = jax.ShapeDtypeStruct((t, d), _BF16)
    vt_shape, vt_spec = _vt_shape_spec(batch, _DIFF_HEADS, seq, tm)
    return pl.pallas_call(
        functools.partial(_diff_qkv_kernel, scale=_DIFF_HEAD_DIM ** -0.5),
        grid=(t // tm,),
        in_specs=[row, pl.BlockSpec((1, d), lambda i: (0, 0)),
                  pl.BlockSpec((d, 2 * d), lambda i: (0, 0)),
                  pl.BlockSpec((d, d), lambda i: (0, 0)), tab, tab, tab],
        out_specs=[row, row, vt_spec],
        out_shape=[out, out, vt_shape],
        compiler_params=_cparams("parallel"),
        name="diff_qkv",
    )(x, g.reshape(1, d), w[:, :2 * d].astype(_BF16), w[:, 2 * d:].T.astype(_BF16), c, su, sd)


def _diff_attn_kernel(q_ref, k_ref, v_ref, lq1_ref, lk1_ref, lq2_ref, lk2_ref, g_ref, o_ref,
                      s_scr, p_scr, acc_scr, *, lambda_init):
    q = q_ref[...]
    lane = lax.broadcasted_iota(_I32, q.shape, 1)
    zero = jnp.zeros_like(q)
    q0 = jnp.where(lane < _DIFF_HEAD_DIM, q, zero)
    q1 = jnp.where(lane >= _DIFF_HEAD_DIM, q, zero)
    o0, o1 = _flash_multi([q0, q1], [k_ref, k_ref], v_ref, s_scr, p_scr, acc_scr)
    lam = (jnp.exp(jnp.sum(lq1_ref[...] * lk1_ref[...], axis=-1, keepdims=True))
           - jnp.exp(jnp.sum(lq2_ref[...] * lk2_ref[...], axis=-1, keepdims=True)) + lambda_init)
    o = o0 - lam * o1
    o = _rmsnorm(o, g_ref[...]) * (1.0 - lambda_init)
    o_ref[...] = o.astype(_BF16)


def _vt_block(vt):
    return pl.BlockSpec((None, None) + vt.shape[2:], lambda b, h, i: (b, h, 0, 0, 0))


def _diff_attn(q, k, vt, lq1, lk1, lq2, lk2, subln_g, batch, seq, lambda_init):
    t, d = q.shape
    tq = _tile(seq, 256)
    tk = vt.shape[-1]
    nq = seq // tq
    qspec = pl.BlockSpec((tq, _LANES), lambda b, h, i: (b * nq + i, h))
    kspec = pl.BlockSpec((seq, _LANES), lambda b, h, i: (b, h))
    small = lambda n: pl.BlockSpec((1, n), lambda b, h, i: (0, 0))
    hd = _DIFF_HEAD_DIM
    return pl.pallas_call(
        functools.partial(_diff_attn_kernel, lambda_init=lambda_init),
        grid=(batch, _DIFF_HEADS, nq),
        in_specs=[qspec, kspec, _vt_block(vt), small(hd), small(hd), small(hd), small(hd),
                  small(2 * hd)],
        out_specs=qspec,
        out_shape=jax.ShapeDtypeStruct((t, d), _BF16),
        scratch_shapes=_flash_scratch(2, tq, tk, _LANES),
        compiler_params=_cparams("parallel", "parallel", "arbitrary"),
        name="diff_attn",
    )(q, k, vt, lq1.reshape(1, hd), lk1.reshape(1, hd), lq2.reshape(1, hd), lk2.reshape(1, hd),
      subln_g.reshape(1, 2 * hd))


def _mla_proj_kernel(x_ref, g_ref, wa_ref, gq_ref, wq_ref, gkv_ref, wk_ref, wvt_ref, c_ref, su_ref,
                     sd_ref, q_ref, k_ref, vt_ref, *, scale):
    h = _rmsnorm(x_ref[...], g_ref[...]).astype(_BF16)
    a = jnp.dot(h, wa_ref[...], preferred_element_type=_F32)
    c, su, sd = c_ref[...], su_ref[...], sd_ref[...]
    half = _MLA_ROPE // 2
    kv_lo = _MLA_Q_RANK + _MLA_KV_RANK
    cq = _rmsnorm(a[:, :_MLA_Q_RANK], gq_ref[...]).astype(_BF16)
    ckv = _rmsnorm(a[:, _MLA_Q_RANK:kv_lo], gkv_ref[...]).astype(_BF16)
    k_rope = _rope_lanes(a[:, kv_lo:kv_lo + _LANES], c, su, sd, half)
    qf = jnp.dot(cq, wq_ref[...], preferred_element_type=_F32)
    kf = jnp.dot(ckv, wk_ref[...], preferred_element_type=_F32)
    for j in range(_MLA_HEADS):
        lo, hi = j * _LANES, (j + 1) * _LANES
        q_ref[:, lo:hi] = (_rope_lanes(qf[:, lo:hi], c, su, sd, half) * scale).astype(_BF16)
        k_ref[:, lo:hi] = (kf[:, lo:hi] + k_rope).astype(_BF16)
    _store_vt(vt_ref, lax.dot_general(wvt_ref[...], ckv, _NT, preferred_element_type=_F32),
              _MLA_HEADS * _MLA_V // _LANES)


def _mla_weights(w_a, w_q_b, w_kv_b):
    d = w_a.shape[0]
    kv_lo = _MLA_Q_RANK + _MLA_KV_RANK
    z = lambda *s: jnp.zeros(s, _F32)
    pad = _LANES - _MLA_NOPE - _MLA_ROPE
    wa = jnp.concatenate([w_a[:, :kv_lo], z(d, _MLA_NOPE), w_a[:, kv_lo:], z(d, pad)], axis=1)
    wq = w_q_b.reshape(_MLA_Q_RANK, _MLA_HEADS, _MLA_NOPE + _MLA_ROPE)
    wq = jnp.concatenate([wq, z(_MLA_Q_RANK, _MLA_HEADS, pad)], axis=2)
    wkv = w_kv_b.reshape(_MLA_KV_RANK, _MLA_HEADS, _MLA_NOPE + _MLA_V)
    wk = jnp.concatenate([wkv[:, :, :_MLA_NOPE], z(_MLA_KV_RANK, _MLA_HEADS, _LANES - _MLA_NOPE)], axis=2)
    wvt = wkv[:, :, _MLA_NOPE:].reshape(_MLA_KV_RANK, -1).T
    return (wa.astype(_BF16), wq.reshape(_MLA_Q_RANK, -1).astype(_BF16),
            wk.reshape(_MLA_KV_RANK, -1).astype(_BF16), wvt.astype(_BF16))


def _mla_proj(x, g, w_a, gq, w_q_b, gkv, w_kv_b, batch, seq):
    t, d = x.shape
    tm = _tile(seq, 512)
    nseq = seq // tm
    c, su, sd = _rope_tables(seq, _LANES, _MLA_NOPE, _MLA_ROPE)
    wa, wq, wk, wvt = _mla_weights(w_a, w_q_b, w_kv_b)
    hq = _MLA_HEADS * _LANES
    tab = pl.BlockSpec((tm, _LANES), lambda i: (i % nseq, 0))
    full = lambda a: pl.BlockSpec(a.shape, lambda i: (0, 0))
    row = lambda n: pl.BlockSpec((tm, n), lambda i: (i, 0))
    g, gq, gkv = g.reshape(1, -1), gq.reshape(1, -1), gkv.reshape(1, -1)
    scale = (_MLA_NOPE + _MLA_ROPE) ** -0.5
    vt_shape, vt_spec = _vt_shape_spec(batch, _MLA_HEADS * _MLA_V // _LANES, seq, tm)
    return pl.pallas_call(
        functools.partial(_mla_proj_kernel, scale=scale),
        grid=(t // tm,),
        in_specs=[row(d), full(g), full(wa), full(gq), full(wq), full(gkv), full(wk), full(wvt),
                  tab, tab, tab],
        out_specs=[row(hq), row(hq), vt_spec],
        out_shape=[jax.ShapeDtypeStruct((t, hq), _BF16), jax.ShapeDtypeStruct((t, hq), _BF16), vt_shape],
        compiler_params=_cparams("parallel"),
        name="mla_proj",
    )(x, g, wa, gq, wq, gkv, wk, wvt, c, su, sd)


def _mla_attn_kernel(q_ref, ka_ref, kb_ref, v_ref, o_ref, s_scr, p_scr, acc_scr):
    oa, ob = _flash_multi([q_ref[:, :_LANES], q_ref[:, _LANES:]], [ka_ref, kb_ref], v_ref,
                          s_scr, p_scr, acc_scr)
    lane = lax.broadcasted_iota(_I32, oa.shape, 1)
    o_ref[...] = jnp.where(lane < _MLA_V, oa, ob).astype(_BF16)


def _mla_attn(q, k, vt, batch, seq):
    t = q.shape[0]
    tq = _tile(seq, 256)
    tk = vt.shape[-1]
    nq = seq // tq
    return pl.pallas_call(
        _mla_attn_kernel,
        grid=(batch, _MLA_HEADS // 2, nq),
        in_specs=[pl.BlockSpec((tq, 2 * _LANES), lambda b, h, i: (b * nq + i, h)),
                  pl.BlockSpec((seq, _LANES), lambda b, h, i: (b, 2 * h)),
                  pl.BlockSpec((seq, _LANES), lambda b, h, i: (b, 2 * h + 1)),
                  _vt_block(vt)],
        out_specs=pl.BlockSpec((tq, _LANES), lambda b, h, i: (b * nq + i, h)),
        out_shape=jax.ShapeDtypeStruct((t, _MLA_HEADS * _MLA_V), _BF16),
        scratch_shapes=_flash_scratch(2, tq, tk, _LANES),
        compiler_params=_cparams("parallel", "parallel", "arbitrary"),
        name="mla_attn",
    )(q, k, k, vt)


def _oproj_router_kernel(o_ref, w_ref, x_ref, g_ref, rwh_ref, rwl_ref, x1_ref, hn_ref, aff_ref):
    x1 = x_ref[...] + jnp.dot(o_ref[...], w_ref[...], preferred_element_type=_F32)
    x1_ref[...] = x1
    hn = _rmsnorm(x1, g_ref[...])
    hn_ref[...] = hn
    hh = hn.astype(_BF16)
    hl = (hn - hh.astype(_F32)).astype(_BF16)
    nt = (((1,), (1,)), ((), ()))
    rwh = rwh_ref[...]
    logits = (lax.dot_general(rwh, hh, nt, preferred_element_type=_F32)
              + lax.dot_general(rwh, hl, nt, preferred_element_type=_F32)
              + lax.dot_general(rwl_ref[...], hh, nt, preferred_element_type=_F32))
    e = jnp.exp(logits - jnp.max(logits, axis=0, keepdims=True))
    aff_ref[...] = e / jnp.sum(e, axis=0, keepdims=True)


def _oproj_router(o, w_o, x, g, router_w):
    t, d = x.shape
    tm = _tile(t, 512)
    ne = router_w.shape[1]
    rwt = router_w.T
    rwh = rwt.astype(_BF16)
    rwl = (rwt - rwh.astype(_F32)).astype(_BF16)
    row = pl.BlockSpec((tm, d), lambda i: (i, 0))
    full = lambda a: pl.BlockSpec(a.shape, lambda i: (0, 0))
    g = g.reshape(1, d)
    w = w_o.astype(_BF16)
    return pl.pallas_call(
        _oproj_router_kernel,
        grid=(t // tm,),
        in_specs=[row, full(w), row, full(g), full(rwh), full(rwl)],
        out_specs=[row, row, pl.BlockSpec((ne, tm), lambda i: (0, i))],
        out_shape=[jax.ShapeDtypeStruct((t, d), _F32), jax.ShapeDtypeStruct((t, d), _F32),
                   jax.ShapeDtypeStruct((ne, t), _F32)],
        compiler_params=_cparams("parallel"),
        name="oproj_router",
    )(o, w, x, g, rwh, rwl)


def _select_kernel(aff_ref, idx_ref, gate_ref, base_ref, *, cap, n_experts, ns):
    rows = aff_ref.shape[0] // n_experts

    def iota(shape, dim):
        return lax.broadcasted_iota(_I32, shape, dim)

    sq = (_LANES, _LANES)
    upto = (iota(sq, 0) <= iota(sq, 1)).astype(_BF16)
    upto_t = (iota(sq, 1) <= iota(sq, 0)).astype(_BF16)
    above = (iota((rows, rows), 1) < iota((rows, rows), 0)).astype(_BF16)
    capf = jnp.float32(cap)
    row_id = iota((rows, ns), 0).astype(_F32)
    lane_id = iota((_LANES, ns), 0).astype(_F32)
    slot_id = iota((1, ns), 1).astype(_F32)

    def lanes(col):
        return jnp.broadcast_to(col, (rows, _LANES))

    def rows_before(tot):
        return jnp.dot(above, lanes(tot).astype(_BF16), preferred_element_type=_F32)[:, :1]

    def expert(e, carry):
        r0 = pl.multiple_of(e * rows, rows)
        aff = aff_ref[pl.ds(r0, rows), :]
        bits = pltpu.bitcast(aff, _I32)

        def step(i, thr):
            cand = thr | lax.shift_left(jnp.int32(1), jnp.int32(30) - lax.convert_element_type(i, _I32))
            return jnp.where(jnp.sum(jnp.where(bits >= cand, 1.0, 0.0), keepdims=True) >= capf, cand, thr)

        thr = lax.fori_loop(0, 31, step, jnp.zeros((1, 1), _I32))
        gt = bits > thr
        eq = jnp.where(bits == thr, 1.0, 0.0)
        need = capf - jnp.sum(jnp.where(gt, 1.0, 0.0), keepdims=True)
        rank = (rows_before(jnp.sum(eq, axis=-1, keepdims=True))
                + jnp.dot(eq.astype(_BF16), upto, preferred_element_type=_F32) - eq)
        sel = jnp.where(gt | ((eq > 0.0) & (rank < need)), 1.0, 0.0)

        tot = jnp.sum(sel, axis=-1, keepdims=True)
        base = rows_before(tot)
        cum = base + tot
        base_ref[pl.ds(r0, rows), :] = lanes(base).astype(_I32)
        incl_t = jnp.dot(upto_t, sel.T.astype(_BF16), preferred_element_type=_F32).astype(_BF16)
        aff_t = aff.T
        a0 = aff_t.astype(_BF16)
        a1 = (aff_t - a0.astype(_F32)).astype(_BF16)
        a2 = (aff_t - a0.astype(_F32) - a1.astype(_F32)).astype(_BF16)
        for c in range(cap // ns):
            j = slot_id + float(c * ns)
            rj = jnp.sum(jnp.where(cum <= j, 1.0, 0.0), axis=0, keepdims=True)
            hit = row_id == rj
            q = j - jnp.sum(jnp.where(hit, base, 0.0), axis=0, keepdims=True)
            onehot = jnp.where(hit, 1.0, 0.0).astype(_BF16)
            counts = jnp.dot(incl_t, onehot, preferred_element_type=_F32)
            lpos = jnp.sum(jnp.where(counts <= q, 1.0, 0.0), axis=0, keepdims=True)
            arow = (jnp.dot(a0, onehot, preferred_element_type=_F32)
                    + jnp.dot(a1, onehot, preferred_element_type=_F32)
                    + jnp.dot(a2, onehot, preferred_element_type=_F32))
            gate = jnp.sum(jnp.where(lane_id == lpos, arow, 0.0), axis=0, keepdims=True)
            idx_ref[e, pl.ds(c, 1), :] = (rj * float(_LANES) + lpos).astype(_I32)
            gate_ref[e, pl.ds(c, 1), :] = gate
        return carry

    lax.fori_loop(0, n_experts, expert, 0)


def _select(aff_t, cap):
    ne, t = aff_t.shape
    rows = t // _LANES
    ns = _tile(cap, 512)
    aff2 = aff_t.reshape(ne * rows, _LANES)
    listing = jax.ShapeDtypeStruct((ne, cap // ns, ns), _I32)
    idx, gates, base = pl.pallas_call(
        functools.partial(_select_kernel, cap=cap, n_experts=ne, ns=ns),
        out_shape=[listing, jax.ShapeDtypeStruct(listing.shape, _F32),
                   jax.ShapeDtypeStruct(aff2.shape, _I32)],
        compiler_params=pltpu.CompilerParams(vmem_limit_bytes=_VMEM_LIMIT),
        name="expert_select",
    )(aff2)
    return idx.reshape(ne, cap), gates.reshape(ne, cap), base.reshape(ne, rows, _LANES)[:, :, 0]


def _ffn_kernel(idx_ref, nxt_ref, hn_hbm, gate_ref, wg_ref, wu_ref, wd_ref, ye_ref, xbuf, sems,
                *, f_chunk):
    tc = xbuf.shape[1]
    n_tiles = pl.num_programs(1)
    step = pl.program_id(0) * n_tiles + pl.program_id(1)
    last = pl.num_programs(0) * n_tiles - 1
    slot = step % 2

    def row_copy(j, t, buf):
        return pltpu.make_async_copy(hn_hbm.at[pl.ds(t, 1)], xbuf.at[buf, pl.ds(j, 1)], sems.at[buf])

    def gather(ids_ref, buf):
        def issue(j, carry):
            row_copy(j, ids_ref[0, 0, j], buf).start()
            return carry
        lax.fori_loop(0, tc, issue, 0, unroll=8)

    @pl.when(step == 0)
    def _():
        gather(idx_ref, 0)

    @pl.when(step < last)
    def _():
        gather(nxt_ref, 1 - slot)

    def drain(j, carry):
        row_copy(j, 0, slot).wait()
        return carry

    lax.fori_loop(0, tc, drain, 0, unroll=True)
    x = xbuf[slot].astype(_BF16)
    d_ff = wg_ref.shape[-1]
    y = jnp.zeros((tc, wd_ref.shape[-1]), _F32)
    for f in range(0, d_ff, f_chunk):
        g = jnp.dot(x, wg_ref[0, :, f:f + f_chunk], preferred_element_type=_F32)
        u = jnp.dot(x, wu_ref[0, :, f:f + f_chunk], preferred_element_type=_F32)
        hid = (g * jax.nn.sigmoid(g) * u).astype(_BF16)
        y = y + jnp.dot(hid, wd_ref[0, f:f + f_chunk, :], preferred_element_type=_F32)
    ye_ref[0] = y * gate_ref[0]


def _expert_ffn(hn, idx, gates, wg, wu, wd):
    t, d = hn.shape
    ne, cap = idx.shape
    d_ff = wg.shape[-1]
    tc = _tile(cap, 512)
    nc = cap // tc
    last = ne * nc - 1
    ids = idx.reshape(ne * nc, 1, tc)
    return pl.pallas_call(
        functools.partial(_ffn_kernel, f_chunk=_tile(d_ff, 512)),
        grid=(ne, nc),
        in_specs=[pl.BlockSpec((1, 1, tc), lambda e, j: (e * nc + j, 0, 0), memory_space=pltpu.SMEM),
                  pl.BlockSpec((1, 1, tc), lambda e, j: (jnp.minimum(e * nc + j + 1, last), 0, 0),
                               memory_space=pltpu.SMEM),
                  pl.BlockSpec(memory_space=pl.ANY),
                  pl.BlockSpec((1, tc, 1), lambda e, j: (e, j, 0)),
                  pl.BlockSpec((1, d, d_ff), lambda e, j: (e, 0, 0)),
                  pl.BlockSpec((1, d, d_ff), lambda e, j: (e, 0, 0)),
                  pl.BlockSpec((1, d_ff, d), lambda e, j: (e, 0, 0))],
        out_specs=pl.BlockSpec((1, tc, d), lambda e, j: (e, j, 0)),
        out_shape=jax.ShapeDtypeStruct((ne, cap, d), _F32),
        scratch_shapes=[pltpu.VMEM((2, tc, d), _F32), pltpu.SemaphoreType.DMA((2,))],
        compiler_params=_cparams("arbitrary", "arbitrary"),
        name="expert_ffn",
    )(ids, ids, hn, gates.reshape(ne, cap, 1), wg, wu, wd)


_ROW_CHUNK = 8
_RMW_GROUP = 4


def _div(x, n):
    if n & (n - 1) == 0:
        return lax.shift_right_logical(x, jnp.int32(n.bit_length() - 1))
    return lax.div(x, jnp.int32(n))


def _combine_kernel(lo_ref, x_ref, ye_hbm, *rest, n_experts, cap, idx_block):
    idx_refs = rest[:2 * n_experts]
    out_ref, stage, sems = rest[2 * n_experts:]
    tb = x_ref.shape[0]
    region = stage.shape[1] // n_experts
    b = pl.program_id(0)
    slot = b % 2

    def for_each_chunk(blk, buf, fn):
        for e in range(n_experts):
            lo, hi = lo_ref[e, blk], lo_ref[e, blk + 1]
            c0 = _div(lo, _ROW_CHUNK)

            def one(c, carry):
                src = pl.multiple_of(e * cap + c * _ROW_CHUNK, _ROW_CHUNK)
                dst = pl.multiple_of(e * region + (c - c0) * _ROW_CHUNK, _ROW_CHUNK)
                fn(pltpu.make_async_copy(ye_hbm.at[pl.ds(src, _ROW_CHUNK)],
                                         stage.at[buf, pl.ds(dst, _ROW_CHUNK)], sems.at[buf]))
                return carry

            lax.fori_loop(c0, _div(hi + (_ROW_CHUNK - 1), _ROW_CHUNK), one, 0)

    @pl.when(b == 0)
    def _():
        for_each_chunk(0, 0, lambda cp: cp.start())

    @pl.when(b + 1 < pl.num_programs(0))
    def _():
        for_each_chunk(b + 1, 1 - slot, lambda cp: cp.start())

    out_ref[...] = x_ref[...]
    for_each_chunk(b, slot, lambda cp: cp.wait())
    for e in range(n_experts):
        lo, hi = lo_ref[e, b], lo_ref[e, b + 1]
        first = _div(lo, idx_block) * idx_block
        row0 = e * region - _div(lo, _ROW_CHUNK) * _ROW_CHUNK
        ia, ib = idx_refs[2 * e], idx_refs[2 * e + 1]

        def token(j):
            k = j - first
            kk = k & (idx_block - 1) if idx_block & (idx_block - 1) == 0 else lax.rem(k, idx_block)
            return jnp.where(k < idx_block, ia[0, 0, 0, kk], ib[0, 0, 0, kk]) - b * tb

        def add_rows(j0, count):
            toks = [token(j0 + i) for i in range(count)]
            sums = [out_ref[pl.ds(toks[i], 1), :] + stage[slot, pl.ds(row0 + j0 + i, 1), :]
                    for i in range(count)]
            for i in range(count):
                out_ref[pl.ds(toks[i], 1), :] = sums[i]

        def group(g, carry):
            add_rows(lo + g * _RMW_GROUP, _RMW_GROUP)
            return carry

        def single(j, carry):
            add_rows(j, 1)
            return carry

        n_groups = _div(hi - lo, _RMW_GROUP)
        lax.fori_loop(0, n_groups, group, 0)
        lax.fori_loop(lo + n_groups * _RMW_GROUP, hi, single, 0)


def _combine(x1, ye, idx, before):
    t, d = x1.shape
    ne, cap = idx.shape
    tb = _tile(t, _LANES)
    nb = t // tb
    assert cap % _ROW_CHUNK == 0
    idx_block = min(tb, cap)
    n_ib = cap // idx_block
    lo = jnp.concatenate([before[:, ::tb // _LANES], jnp.full((ne, 1), cap, _I32)], axis=1)
    idx4 = idx.reshape(ne, n_ib, 1, idx_block)

    def idx_spec(e, w):
        return pl.BlockSpec(
            (1, 1, 1, idx_block),
            lambda b, lo_ref: (e, jnp.minimum(_div(lo_ref[e, b], idx_block) + w, n_ib - 1), 0, 0),
            memory_space=pltpu.SMEM)

    region = tb + _ROW_CHUNK
    row = pl.BlockSpec((tb, d), lambda b, lo_ref: (b, 0))
    grid_spec = pltpu.PrefetchScalarGridSpec(
        num_scalar_prefetch=1,
        grid=(nb,),
        in_specs=[row, pl.BlockSpec(memory_space=pl.ANY)]
                 + [idx_spec(e, w) for e in range(ne) for w in range(2)],
        out_specs=row,
        scratch_shapes=[pltpu.VMEM((2, ne * region, d), _F32), pltpu.SemaphoreType.DMA((2,))])
    return pl.pallas_call(
        functools.partial(_combine_kernel, n_experts=ne, cap=cap, idx_block=idx_block),
        grid_spec=grid_spec,
        out_shape=jax.ShapeDtypeStruct((t, d), _F32),
        compiler_params=_cparams("arbitrary"),
        name="moe_combine",
    )(lo, x1, ye.reshape(ne * cap, d), *([idx4] * (2 * ne)))


def _final_norm_kernel(x_ref, g_ref, o_ref):
    o_ref[...] = _rmsnorm(x_ref[...], g_ref[...])


def _final_norm(x, g):
    t, d = x.shape
    tm = _tile(t, 1024)
    row = pl.BlockSpec((tm, d), lambda i: (i, 0))
    return pl.pallas_call(
        _final_norm_kernel,
        grid=(t // tm,),
        in_specs=[row, pl.BlockSpec((1, d), lambda i: (0, 0))],
        out_specs=row,
        out_shape=jax.ShapeDtypeStruct((t, d), _F32),
        compiler_params=_cparams("parallel"),
        name="final_norm",
    )(x, g.reshape(1, d))


def _moe(x1, hn, aff_t, wg, wu, wd):
    t, d = x1.shape
    ne = aff_t.shape[0]
    cap = max(1, _CAPACITY_FACTOR * t // ne)
    idx, gates, before = _select(aff_t, cap)
    ye = _expert_ffn(hn, idx, gates, wg, wu, wd)
    return _combine(x1, ye, idx, before)


def _trunk(x, p):
    batch, seq, d = x.shape
    x = x.reshape(batch * seq, d)
    depth = p["ffn_norm_g"].shape[0]
    for i in range(depth):
        j = i // _N_MIXERS
        if i % _N_MIXERS == 0:
            lambda_init = 0.8 - 0.6 * float(np.exp(-0.3 * i))
            q, k, v = _diff_qkv(x, p["diff_norm_g"][j], p["diff_w_qkv"][j], batch, seq)
            o = _diff_attn(q, k, v, p["diff_lambda_q1"][j], p["diff_lambda_k1"][j],
                           p["diff_lambda_q2"][j], p["diff_lambda_k2"][j], p["diff_subln_g"][j],
                           batch, seq, lambda_init)
            w_o = p["diff_w_o"][j]
        else:
            q, k, v = _mla_proj(x, p["mla_norm_g"][j], p["mla_w_a"][j], p["mla_q_norm_g"][j],
                                p["mla_w_q_b"][j], p["mla_kv_norm_g"][j], p["mla_w_kv_b"][j],
                                batch, seq)
            o = _mla_attn(q, k, v, batch, seq)
            w_o = p["mla_w_o"][j]
        x1, hn, aff_t = _oproj_router(o, w_o, x, p["ffn_norm_g"][i], p["router_w"][i])
        x = _moe(x1, hn, aff_t, p["w_gate_bf16"][i], p["w_up_bf16"][i], p["w_down_bf16"][i])
    return _final_norm(x, p["final_norm_g"]).reshape(batch, seq, d)


def kernel(x_prompt, x_sample, diff_norm_g, diff_w_qkv, diff_lambda_q1, diff_lambda_k1, diff_lambda_q2,
           diff_lambda_k2, diff_subln_g, diff_w_o, mla_norm_g, mla_w_a, mla_q_norm_g, mla_w_q_b,
           mla_kv_norm_g, mla_w_kv_b, mla_w_o, ffn_norm_g, router_w, w_gate, w_up, w_down, final_norm_g):
    p = dict(diff_norm_g=diff_norm_g, diff_w_qkv=diff_w_qkv, diff_lambda_q1=diff_lambda_q1,
             diff_lambda_k1=diff_lambda_k1, diff_lambda_q2=diff_lambda_q2, diff_lambda_k2=diff_lambda_k2,
             diff_subln_g=diff_subln_g, diff_w_o=diff_w_o, mla_norm_g=mla_norm_g, mla_w_a=mla_w_a,
             mla_q_norm_g=mla_q_norm_g, mla_w_q_b=mla_w_q_b, mla_kv_norm_g=mla_kv_norm_g,
             mla_w_kv_b=mla_w_kv_b, mla_w_o=mla_w_o, ffn_norm_g=ffn_norm_g, router_w=router_w,
             final_norm_g=final_norm_g, w_gate_bf16=w_gate.astype(_BF16), w_up_bf16=w_up.astype(_BF16),
             w_down_bf16=w_down.astype(_BF16))
    return _trunk(x_prompt, p), _trunk(x_sample, p)
```

```python
import functools

import numpy as np
import jax
import jax.numpy as jnp
from jax import lax
from jax.experimental import pallas as pl
from jax.experimental.pallas import tpu as pltpu

_F32, _BF16, _I32 = jnp.float32, jnp.bfloat16, jnp.int32
_EPS = 1e-6
_ROPE_THETA = 500000.0
_LANES = 128
_NEG = -1e30
_VMEM_LIMIT = 56 * 1024 * 1024

_DIFF_HEADS, _DIFF_HEAD_DIM, _DIFF_ROT = 8, 64, 16
_MLA_HEADS, _MLA_NOPE, _MLA_ROPE, _MLA_V = 16, 64, 32, 64
_MLA_Q_RANK, _MLA_KV_RANK = 384, 256
_N_EXPERTS, _CAPACITY_FACTOR = 16, 2
_N_MIXERS = 2


def _cparams(*sem):
    return pltpu.CompilerParams(dimension_semantics=sem, vmem_limit_bytes=_VMEM_LIMIT)


def _tile(n, pref):
    t = min(n, pref)
    assert n % t == 0, (n, pref)
    return t


def _rmsnorm(x, g):
    return x * lax.rsqrt(jnp.mean(x * x, axis=-1, keepdims=True) + _EPS) * g


def _rope_lanes(y, c, s_up, s_dn, half):
    return y * c + pltpu.roll(y, _LANES - half, 1) * s_up + pltpu.roll(y, half, 1) * s_dn


def _rope_tables(seq, group, start, rot):
    half = rot // 2
    pos = jnp.arange(seq, dtype=_F32)
    inv = jnp.float32(_ROPE_THETA) ** (-jnp.arange(0, rot, 2, dtype=_F32) / rot)
    ang = pos[:, None] * inv[None, :]
    cos, sin = jnp.cos(ang), jnp.sin(ang)
    j = (np.arange(_LANES) % group) - start
    first = (j >= 0) & (j < half)
    second = (j >= half) & (j < rot)
    f = np.where(first, j, np.where(second, j - half, 0))
    c = jnp.where(first | second, cos[:, f], 1.0)
    s_up = jnp.where(first, -sin[:, f], 0.0)
    s_dn = jnp.where(second, sin[:, f], 0.0)
    return c, s_up, s_dn


_LOG2E = 1.4426950408889634
_ONES_ROWS = 16
_CHUNK_UNROLL = 2


def _query_tiles(seq):
    tqs = _tile(seq, 256)
    return _tile(seq, 2 * tqs), tqs


def _flash_scratch(n, tq, tk, dv):
    return [pltpu.VMEM((n, tk, tq), _F32), pltpu.VMEM((n, tk, tq), _BF16),
            pltpu.VMEM((n, dv + _ONES_ROWS, tq), _F32)]


def _flash_multi(qs, k_refs, vt_ref, s_scr, p_scr, acc_scr):
    n, tk, tq = s_scr.shape
    n_chunks = vt_ref.shape[0]
    dv = vt_ref.shape[1] - _ONES_ROWS

    def scores(j):
        out = []
        for c in range(n):
            ks = k_refs[c][pl.ds(pl.multiple_of(j * tk, tk), tk), :]
            s = lax.dot_general(ks, qs[c], (((1,), (1,)), ((), ())), preferred_element_type=_F32)
            s_scr[c] = s
            out.append(jnp.max(s, axis=0, keepdims=True))
        return out

    def softmax(smax, st):
        out = []
        for c in range(n):
            m_new = jnp.maximum(st[c][0], smax[c])
            alpha = jnp.exp2(st[c][0] - m_new)
            p_scr[c] = jnp.exp2(s_scr[c] - m_new).astype(_BF16)
            out.append((m_new, alpha))
        return out

    def accumulate(st, j):
        for c in range(n):
            acc_scr[c] = st[c][1] * acc_scr[c] + jnp.dot(vt_ref[j], p_scr[c],
                                                         preferred_element_type=_F32)

    acc_scr[...] = jnp.zeros(acc_scr.shape, _F32)
    st = [(jnp.full((1, tq), _NEG, _F32), None)] * n
    st = softmax(scores(0), st)
    if n_chunks > 1:
        def body(j, carry):
            smax, st = carry
            accumulate(st, j - 1)
            st = softmax(smax, st)
            return scores(j + 1), st

        smax, st = lax.fori_loop(1, n_chunks - 1, body, (scores(1), st), unroll=_CHUNK_UNROLL)
        accumulate(st, n_chunks - 2)
        st = softmax(smax, st)
    accumulate(st, n_chunks - 1)
    return [(acc_scr[c, :dv, :] * (1.0 / acc_scr[c, dv:dv + 1, :])).T for c in range(n)]


_NT = (((1,), (1,)), ((), ()))


def _store_vt(vt_ref, vt, heads):
    ones = jnp.ones((_ONES_ROWS, vt.shape[1]), _BF16)
    for j in range(heads):
        vt_ref[0, j, 0, :_LANES, :] = vt[j * _LANES:(j + 1) * _LANES, :].astype(_BF16)
        vt_ref[0, j, 0, _LANES:, :] = ones


def _diff_qkv_kernel(x_ref, g_ref, w_ref, wvt_ref, c_ref, su_ref, sd_ref, q_ref, k_ref, vt_ref,
                     *, scale):
    h = _rmsnorm(x_ref[...], g_ref[...]).astype(_BF16)
    y = jnp.dot(h, w_ref[...], preferred_element_type=_F32)
    d = q_ref.shape[-1]
    c, su, sd = c_ref[...], su_ref[...], sd_ref[...]
    half = _DIFF_ROT // 2
    for j in range(d // _LANES):
        lo, hi = j * _LANES, (j + 1) * _LANES
        q_ref[:, lo:hi] = (_rope_lanes(y[:, lo:hi], c, su, sd, half) * scale).astype(_BF16)
        k_ref[:, lo:hi] = _rope_lanes(y[:, d + lo:d + hi], c, su, sd, half).astype(_BF16)
    _store_vt(vt_ref, lax.dot_general(wvt_ref[...], h, _NT, preferred_element_type=_F32), _DIFF_HEADS)


def _vt_shape_spec(batch, heads, seq, tm):
    nseq = seq // tm
    rows = _LANES + _ONES_ROWS
    return (jax.ShapeDtypeStruct((batch, heads, nseq, rows, tm), _BF16),
            pl.BlockSpec((1, heads, 1, rows, tm), lambda i: (i // nseq, 0, i % nseq, 0, 0)))


def _diff_qkv(x, g, w, batch, seq):
    t, d = x.shape
    tm = _tile(seq, 512)
    c, su, sd = _rope_tables(seq, _DIFF_HEAD_DIM, 0, _DIFF_ROT)
    nseq = seq // tm
    tab = pl.BlockSpec((tm, _LANES), lambda i: (i % nseq, 0))
    row = pl.BlockSpec((tm, d), lambda i: (i, 0))
    out = jax.ShapeDtypeStruct((t, d), _BF16)
    vt_shape, vt_spec = _vt_shape_spec(batch, _DIFF_HEADS, seq, tm)
    return pl.pallas_call(
        functools.partial(_diff_qkv_kernel, scale=_DIFF_HEAD_DIM ** -0.5 * _LOG2E),
        grid=(t // tm,),
        in_specs=[row, pl.BlockSpec((1, d), lambda i: (0, 0)),
                  pl.BlockSpec((d, 2 * d), lambda i: (0, 0)),
                  pl.BlockSpec((d, d), lambda i: (0, 0)), tab, tab, tab],
        out_specs=[row, row, vt_spec],
        out_shape=[out, out, vt_shape],
        compiler_params=_cparams("parallel"),
        name="diff_qkv",
    )(x, g.reshape(1, d), w[:, :2 * d].astype(_BF16), w[:, 2 * d:].T.astype(_BF16), c, su, sd)


def _diff_attn_kernel(q_ref, k_ref, v_ref, lq1_ref, lk1_ref, lq2_ref, lk2_ref, g_ref, o_ref,
                      s_scr, p_scr, acc_scr, *, lambda_init):
    tqs = s_scr.shape[-1]
    qs = []
    for r in range(q_ref.shape[0] // tqs):
        q = q_ref[r * tqs:(r + 1) * tqs, :]
        lane = lax.broadcasted_iota(_I32, q.shape, 1)
        zero = jnp.zeros_like(q)
        qs += [jnp.where(lane < _DIFF_HEAD_DIM, q, zero), jnp.where(lane >= _DIFF_HEAD_DIM, q, zero)]
    outs = _flash_multi(qs, [k_ref] * len(qs), v_ref, s_scr, p_scr, acc_scr)
    lam = (jnp.exp(jnp.sum(lq1_ref[...] * lk1_ref[...], axis=-1, keepdims=True))
           - jnp.exp(jnp.sum(lq2_ref[...] * lk2_ref[...], axis=-1, keepdims=True)) + lambda_init)
    for r in range(len(qs) // 2):
        o = outs[2 * r] - lam * outs[2 * r + 1]
        o = _rmsnorm(o, g_ref[...]) * (1.0 - lambda_init)
        o_ref[r * tqs:(r + 1) * tqs, :] = o.astype(_BF16)


def _vt_block(vt):
    return pl.BlockSpec((None, None) + vt.shape[2:], lambda b, h, i: (b, h, 0, 0, 0))


def _diff_attn(q, k, vt, lq1, lk1, lq2, lk2, subln_g, batch, seq, lambda_init):
    t, d = q.shape
    tq, tqs = _query_tiles(seq)
    tk = vt.shape[-1]
    nq = seq // tq
    qspec = pl.BlockSpec((tq, _LANES), lambda b, h, i: (b * nq + i, h))
    kspec = pl.BlockSpec((seq, _LANES), lambda b, h, i: (b, h))
    small = lambda n: pl.BlockSpec((1, n), lambda b, h, i: (0, 0))
    hd = _DIFF_HEAD_DIM
    return pl.pallas_call(
        functools.partial(_diff_attn_kernel, lambda_init=lambda_init),
        grid=(batch, _DIFF_HEADS, nq),
        in_specs=[qspec, kspec, _vt_block(vt), small(hd), small(hd), small(hd), small(hd),
                  small(2 * hd)],
        out_specs=qspec,
        out_shape=jax.ShapeDtypeStruct((t, d), _BF16),
        scratch_shapes=_flash_scratch(2 * (tq // tqs), tqs, tk, _LANES),
        compiler_params=_cparams("parallel", "parallel", "arbitrary"),
        name="diff_attn",
    )(q, k, vt, lq1.reshape(1, hd), lk1.reshape(1, hd), lq2.reshape(1, hd), lk2.reshape(1, hd),
      subln_g.reshape(1, 2 * hd))


def _mla_proj_kernel(x_ref, g_ref, wa_ref, gq_ref, wq_ref, gkv_ref, wk_ref, wvt_ref, c_ref, su_ref,
                     sd_ref, q_ref, k_ref, vt_ref, *, scale):
    h = _rmsnorm(x_ref[...], g_ref[...]).astype(_BF16)
    a = jnp.dot(h, wa_ref[...], preferred_element_type=_F32)
    c, su, sd = c_ref[...], su_ref[...], sd_ref[...]
    half = _MLA_ROPE // 2
    kv_lo = _MLA_Q_RANK + _MLA_KV_RANK
    cq = _rmsnorm(a[:, :_MLA_Q_RANK], gq_ref[...]).astype(_BF16)
    ckv = _rmsnorm(a[:, _MLA_Q_RANK:kv_lo], gkv_ref[...]).astype(_BF16)
    k_rope = _rope_lanes(a[:, kv_lo:kv_lo + _LANES], c, su, sd, half)
    qf = jnp.dot(cq, wq_ref[...], preferred_element_type=_F32)
    kf = jnp.dot(ckv, wk_ref[...], preferred_element_type=_F32)
    for j in range(_MLA_HEADS):
        lo, hi = j * _LANES, (j + 1) * _LANES
        q_ref[:, lo:hi] = (_rope_lanes(qf[:, lo:hi], c, su, sd, half) * scale).astype(_BF16)
        k_ref[:, lo:hi] = (kf[:, lo:hi] + k_rope).astype(_BF16)
    _store_vt(vt_ref, lax.dot_general(wvt_ref[...], ckv, _NT, preferred_element_type=_F32),
              _MLA_HEADS * _MLA_V // _LANES)


def _mla_weights(w_a, w_q_b, w_kv_b):
    d = w_a.shape[0]
    kv_lo = _MLA_Q_RANK + _MLA_KV_RANK
    z = lambda *s: jnp.zeros(s, _F32)
    pad = _LANES - _MLA_NOPE - _MLA_ROPE
    wa = jnp.concatenate([w_a[:, :kv_lo], z(d, _MLA_NOPE), w_a[:, kv_lo:], z(d, pad)], axis=1)
    wq = w_q_b.reshape(_MLA_Q_RANK, _MLA_HEADS, _MLA_NOPE + _MLA_ROPE)
    wq = jnp.concatenate([wq, z(_MLA_Q_RANK, _MLA_HEADS, pad)], axis=2)
    wkv = w_kv_b.reshape(_MLA_KV_RANK, _MLA_HEADS, _MLA_NOPE + _MLA_V)
    wk = jnp.concatenate([wkv[:, :, :_MLA_NOPE], z(_MLA_KV_RANK, _MLA_HEADS, _LANES - _MLA_NOPE)], axis=2)
    wvt = wkv[:, :, _MLA_NOPE:].reshape(_MLA_KV_RANK, -1).T
    return (wa.astype(_BF16), wq.reshape(_MLA_Q_RANK, -1).astype(_BF16),
            wk.reshape(_MLA_KV_RANK, -1).astype(_BF16), wvt.astype(_BF16))


def _mla_proj(x, g, w_a, gq, w_q_b, gkv, w_kv_b, batch, seq):
    t, d = x.shape
    tm = _tile(seq, 512)
    nseq = seq // tm
    c, su, sd = _rope_tables(seq, _LANES, _MLA_NOPE, _MLA_ROPE)
    wa, wq, wk, wvt = _mla_weights(w_a, w_q_b, w_kv_b)
    hq = _MLA_HEADS * _LANES
    tab = pl.BlockSpec((tm, _LANES), lambda i: (i % nseq, 0))
    full = lambda a: pl.BlockSpec(a.shape, lambda i: (0, 0))
    row = lambda n: pl.BlockSpec((tm, n), lambda i: (i, 0))
    g, gq, gkv = g.reshape(1, -1), gq.reshape(1, -1), gkv.reshape(1, -1)
    scale = (_MLA_NOPE + _MLA_ROPE) ** -0.5 * _LOG2E
    vt_shape, vt_spec = _vt_shape_spec(batch, _MLA_HEADS * _MLA_V // _LANES, seq, tm)
    return pl.pallas_call(
        functools.partial(_mla_proj_kernel, scale=scale),
        grid=(t // tm,),
        in_specs=[row(d), full(g), full(wa), full(gq), full(wq), full(gkv), full(wk), full(wvt),
                  tab, tab, tab],
        out_specs=[row(hq), row(hq), vt_spec],
        out_shape=[jax.ShapeDtypeStruct((t, hq), _BF16), jax.ShapeDtypeStruct((t, hq), _BF16), vt_shape],
        compiler_params=_cparams("parallel"),
        name="mla_proj",
    )(x, g, wa, gq, wq, gkv, wk, wvt, c, su, sd)


def _mla_attn_kernel(q_ref, ka_ref, kb_ref, v_ref, o_ref, s_scr, p_scr, acc_scr):
    tqs = s_scr.shape[-1]
    n_rows = q_ref.shape[0] // tqs
    qs = []
    for r in range(n_rows):
        qs += [q_ref[r * tqs:(r + 1) * tqs, :_LANES], q_ref[r * tqs:(r + 1) * tqs, _LANES:]]
    outs = _flash_multi(qs, [ka_ref, kb_ref] * n_rows, v_ref, s_scr, p_scr, acc_scr)
    lane = lax.broadcasted_iota(_I32, outs[0].shape, 1)
    for r in range(n_rows):
        o_ref[r * tqs:(r + 1) * tqs, :] = jnp.where(lane < _MLA_V, outs[2 * r],
                                                    outs[2 * r + 1]).astype(_BF16)


def _mla_attn(q, k, vt, batch, seq):
    t = q.shape[0]
    tq, tqs = _query_tiles(seq)
    tk = vt.shape[-1]
    nq = seq // tq
    return pl.pallas_call(
        _mla_attn_kernel,
        grid=(batch, _MLA_HEADS // 2, nq),
        in_specs=[pl.BlockSpec((tq, 2 * _LANES), lambda b, h, i: (b * nq + i, h)),
                  pl.BlockSpec((seq, _LANES), lambda b, h, i: (b, 2 * h)),
                  pl.BlockSpec((seq, _LANES), lambda b, h, i: (b, 2 * h + 1)),
                  _vt_block(vt)],
        out_specs=pl.BlockSpec((tq, _LANES), lambda b, h, i: (b * nq + i, h)),
        out_shape=jax.ShapeDtypeStruct((t, _MLA_HEADS * _MLA_V), _BF16),
        scratch_shapes=_flash_scratch(2 * (tq // tqs), tqs, tk, _LANES),
        compiler_params=_cparams("parallel", "parallel", "arbitrary"),
        name="mla_attn",
    )(q, k, k, vt)


def _oproj_router_kernel(o_ref, w_ref, x_ref, g_ref, rwh_ref, rwl_ref, x1_ref, hn_ref, aff_ref):
    x1 = x_ref[...] + jnp.dot(o_ref[...], w_ref[...], preferred_element_type=_F32)
    x1_ref[...] = x1
    hn = _rmsnorm(x1, g_ref[...])
    hn_ref[...] = hn
    hh = hn.astype(_BF16)
    hl = (hn - hh.astype(_F32)).astype(_BF16)
    nt = (((1,), (1,)), ((), ()))
    rwh = rwh_ref[...]
    logits = (lax.dot_general(rwh, hh, nt, preferred_element_type=_F32)
              + lax.dot_general(rwh, hl, nt, preferred_element_type=_F32)
              + lax.dot_general(rwl_ref[...], hh, nt, preferred_element_type=_F32))
    e = jnp.exp(logits - jnp.max(logits, axis=0, keepdims=True))
    aff_ref[...] = e / jnp.sum(e, axis=0, keepdims=True)


def _oproj_router(o, w_o, x, g, router_w):
    t, d = x.shape
    tm = _tile(t, 512)
    ne = router_w.shape[1]
    rwt = router_w.T
    rwh = rwt.astype(_BF16)
    rwl = (rwt - rwh.astype(_F32)).astype(_BF16)
    row = pl.BlockSpec((tm, d), lambda i: (i, 0))
    full = lambda a: pl.BlockSpec(a.shape, lambda i: (0, 0))
    g = g.reshape(1, d)
    w = w_o.astype(_BF16)
    return pl.pallas_call(
        _oproj_router_kernel,
        grid=(t // tm,),
        in_specs=[row, full(w), row, full(g), full(rwh), full(rwl)],
        out_specs=[row, row, pl.BlockSpec((ne, tm), lambda i: (0, i))],
        out_shape=[jax.ShapeDtypeStruct((t, d), _F32), jax.ShapeDtypeStruct((t, d), _F32),
                   jax.ShapeDtypeStruct((ne, t), _F32)],
        compiler_params=_cparams("parallel"),
        name="oproj_router",
    )(o, w, x, g, rwh, rwl)


def _select_kernel(aff_ref, idx_ref, gate_ref, base_ref, *, cap, n_experts, ns):
    rows = aff_ref.shape[0] // n_experts

    def iota(shape, dim):
        return lax.broadcasted_iota(_I32, shape, dim)

    sq = (_LANES, _LANES)
    upto = (iota(sq, 0) <= iota(sq, 1)).astype(_BF16)
    upto_t = (iota(sq, 1) <= iota(sq, 0)).astype(_BF16)
    above = (iota((rows, rows), 1) < iota((rows, rows), 0)).astype(_BF16)
    capf = jnp.float32(cap)
    row_id = iota((rows, ns), 0).astype(_F32)
    lane_id = iota((_LANES, ns), 0).astype(_F32)
    slot_id = iota((1, ns), 1).astype(_F32)

    def lanes(col):
        return jnp.broadcast_to(col, (rows, _LANES))

    def rows_before(tot):
        return jnp.dot(above, lanes(tot).astype(_BF16), preferred_element_type=_F32)[:, :1]

    def expert(e, carry):
        r0 = pl.multiple_of(e * rows, rows)
        aff = aff_ref[pl.ds(r0, rows), :]
        bits = pltpu.bitcast(aff, _I32)

        def step(i, thr):
            cand = thr | lax.shift_left(jnp.int32(1), jnp.int32(30) - lax.convert_element_type(i, _I32))
            return jnp.where(jnp.sum(jnp.where(bits >= cand, 1.0, 0.0), keepdims=True) >= capf, cand, thr)

        thr = lax.fori_loop(0, 31, step, jnp.zeros((1, 1), _I32))
        gt = bits > thr
        eq = jnp.where(bits == thr, 1.0, 0.0)
        need = capf - jnp.sum(jnp.where(gt, 1.0, 0.0), keepdims=True)
        rank = (rows_before(jnp.sum(eq, axis=-1, keepdims=True))
                + jnp.dot(eq.astype(_BF16), upto, preferred_element_type=_F32) - eq)
        sel = jnp.where(gt | ((eq > 0.0) & (rank < need)), 1.0, 0.0)

        tot = jnp.sum(sel, axis=-1, keepdims=True)
        base = rows_before(tot)
        cum = base + tot
        base_ref[pl.ds(r0, rows), :] = lanes(base).astype(_I32)
        incl_t = jnp.dot(upto_t, sel.T.astype(_BF16), preferred_element_type=_F32).astype(_BF16)
        aff_t = aff.T
        a0 = aff_t.astype(_BF16)
        a1 = (aff_t - a0.astype(_F32)).astype(_BF16)
        a2 = (aff_t - a0.astype(_F32) - a1.astype(_F32)).astype(_BF16)
        for c in range(cap // ns):
            j = slot_id + float(c * ns)
            rj = jnp.sum(jnp.where(cum <= j, 1.0, 0.0), axis=0, keepdims=True)
            hit = row_id == rj
            q = j - jnp.sum(jnp.where(hit, base, 0.0), axis=0, keepdims=True)
            onehot = jnp.where(hit, 1.0, 0.0).astype(_BF16)
            counts = jnp.dot(incl_t, onehot, preferred_element_type=_F32)
            lpos = jnp.sum(jnp.where(counts <= q, 1.0, 0.0), axis=0, keepdims=True)
            arow = (jnp.dot(a0, onehot, preferred_element_type=_F32)
                    + jnp.dot(a1, onehot, preferred_element_type=_F32)
                    + jnp.dot(a2, onehot, preferred_element_type=_F32))
            gate = jnp.sum(jnp.where(lane_id == lpos, arow, 0.0), axis=0, keepdims=True)
            idx_ref[e, pl.ds(c, 1), :] = (rj * float(_LANES) + lpos).astype(_I32)
            gate_ref[e, pl.ds(c, 1), :] = gate
        return carry

    lax.fori_loop(0, n_experts, expert, 0)


def _select(aff_t, cap):
    ne, t = aff_t.shape
    rows = t // _LANES
    ns = _tile(cap, 512)
    aff2 = aff_t.reshape(ne * rows, _LANES)
    listing = jax.ShapeDtypeStruct((ne, cap // ns, ns), _I32)
    idx, gates, base = pl.pallas_call(
        functools.partial(_select_kernel, cap=cap, n_experts=ne, ns=ns),
        out_shape=[listing, jax.ShapeDtypeStruct(listing.shape, _F32),
                   jax.ShapeDtypeStruct(aff2.shape, _I32)],
        compiler_params=pltpu.CompilerParams(vmem_limit_bytes=_VMEM_LIMIT),
        name="expert_select",
    )(aff2)
    return idx.reshape(ne, cap), gates.reshape(ne, cap), base.reshape(ne, rows, _LANES)[:, :, 0]


def _ffn_kernel(idx_ref, nxt_ref, hn_hbm, gate_ref, wg_ref, wu_ref, wd_ref, ye_ref, xbuf, sems,
                *, f_chunk):
    tc = xbuf.shape[1]
    n_tiles = pl.num_programs(1)
    step = pl.program_id(0) * n_tiles + pl.program_id(1)
    last = pl.num_programs(0) * n_tiles - 1
    slot = step % 2

    def row_copy(j, t, buf):
        return pltpu.make_async_copy(hn_hbm.at[pl.ds(t, 1)], xbuf.at[buf, pl.ds(j, 1)], sems.at[buf])

    def gather(ids_ref, buf, unroll):
        def issue(j, carry):
            row_copy(j, ids_ref[0, 0, j], buf).start()
            return carry
        lax.fori_loop(0, tc, issue, 0, unroll=unroll)

    @pl.when(step == 0)
    def _():
        gather(idx_ref, 0, 8)

    @pl.when(step < last)
    def _():
        gather(nxt_ref, 1 - slot, True)

    def drain(j, carry):
        row_copy(j, 0, slot).wait()
        return carry

    lax.fori_loop(0, tc, drain, 0, unroll=True)
    x = xbuf[slot].astype(_BF16)
    d_ff = wg_ref.shape[-1]
    y = jnp.zeros((tc, wd_ref.shape[-1]), _F32)
    for f in range(0, d_ff, f_chunk):
        g = jnp.dot(x, wg_ref[0, :, f:f + f_chunk], preferred_element_type=_F32)
        u = jnp.dot(x, wu_ref[0, :, f:f + f_chunk], preferred_element_type=_F32)
        hid = (g * jax.nn.sigmoid(g) * u).astype(_BF16)
        y = y + jnp.dot(hid, wd_ref[0, f:f + f_chunk, :], preferred_element_type=_F32)
    ye_ref[0] = y * gate_ref[0]


def _expert_ffn(hn, idx, gates, wg, wu, wd):
    t, d = hn.shape
    ne, cap = idx.shape
    d_ff = wg.shape[-1]
    tc = _tile(cap, 512)
    nc = cap // tc
    last = ne * nc - 1
    ids = idx.reshape(ne * nc, 1, tc)
    return pl.pallas_call(
        functools.partial(_ffn_kernel, f_chunk=_tile(d_ff, 512)),
        grid=(ne, nc),
        in_specs=[pl.BlockSpec((1, 1, tc), lambda e, j: (e * nc + j, 0, 0), memory_space=pltpu.SMEM),
                  pl.BlockSpec((1, 1, tc), lambda e, j: (jnp.minimum(e * nc + j + 1, last), 0, 0),
                               memory_space=pltpu.SMEM),
                  pl.BlockSpec(memory_space=pl.ANY),
                  pl.BlockSpec((1, tc, 1), lambda e, j: (e, j, 0)),
                  pl.BlockSpec((1, d, d_ff), lambda e, j: (e, 0, 0)),
                  pl.BlockSpec((1, d, d_ff), lambda e, j: (e, 0, 0)),
                  pl.BlockSpec((1, d_ff, d), lambda e, j: (e, 0, 0))],
        out_specs=pl.BlockSpec((1, tc, d), lambda e, j: (e, j, 0)),
        out_shape=jax.ShapeDtypeStruct((ne, cap, d), _F32),
        scratch_shapes=[pltpu.VMEM((2, tc, d), _F32), pltpu.SemaphoreType.DMA((2,))],
        compiler_params=_cparams("arbitrary", "arbitrary"),
        name="expert_ffn",
    )(ids, ids, hn, gates.reshape(ne, cap, 1), wg, wu, wd)


_ROW_CHUNK = 8
_RMW_GROUP = 4


def _div(x, n):
    if n & (n - 1) == 0:
        return lax.shift_right_logical(x, jnp.int32(n.bit_length() - 1))
    return lax.div(x, jnp.int32(n))


def _combine_kernel(lo_ref, x_ref, ye_hbm, *rest, n_experts, cap, idx_block):
    idx_refs = rest[:n_experts]
    out_ref, stage, sems = rest[n_experts:]
    tb = x_ref.shape[0]
    region = stage.shape[1] // n_experts
    b = pl.program_id(0)
    slot = b % 2

    def for_each_chunk(blk, buf, fn):
        for e in range(n_experts):
            lo, hi = lo_ref[e, blk], lo_ref[e, blk + 1]
            c0 = _div(lo, _ROW_CHUNK)

            def one(c, carry):
                src = pl.multiple_of(e * cap + c * _ROW_CHUNK, _ROW_CHUNK)
                dst = pl.multiple_of(e * region + (c - c0) * _ROW_CHUNK, _ROW_CHUNK)
                fn(pltpu.make_async_copy(ye_hbm.at[pl.ds(src, _ROW_CHUNK)],
                                         stage.at[buf, pl.ds(dst, _ROW_CHUNK)], sems.at[buf]))
                return carry

            lax.fori_loop(c0, _div(hi + (_ROW_CHUNK - 1), _ROW_CHUNK), one, 0)

    @pl.when(b == 0)
    def _():
        for_each_chunk(0, 0, lambda cp: cp.start())

    @pl.when(b + 1 < pl.num_programs(0))
    def _():
        for_each_chunk(b + 1, 1 - slot, lambda cp: cp.start())

    out_ref[...] = x_ref[...]
    for_each_chunk(b, slot, lambda cp: cp.wait())
    for e in range(n_experts):
        lo, hi = lo_ref[e, b], lo_ref[e, b + 1]
        first = _div(lo, idx_block) * idx_block
        row0 = e * region - _div(lo, _ROW_CHUNK) * _ROW_CHUNK
        ids = idx_refs[e]

        def token(j):
            return ids[0, 0, 0, j - first] - b * tb

        def add_rows(j0, count):
            toks = [token(j0 + i) for i in range(count)]
            sums = [out_ref[pl.ds(toks[i], 1), :] + stage[slot, pl.ds(row0 + j0 + i, 1), :]
                    for i in range(count)]
            for i in range(count):
                out_ref[pl.ds(toks[i], 1), :] = sums[i]

        def group(g, carry):
            add_rows(lo + g * _RMW_GROUP, _RMW_GROUP)
            return carry

        def single(j, carry):
            add_rows(j, 1)
            return carry

        n_groups = _div(hi - lo, _RMW_GROUP)
        lax.fori_loop(0, n_groups, group, 0)
        lax.fori_loop(lo + n_groups * _RMW_GROUP, hi, single, 0)


def _combine(x1, ye, idx, before):
    t, d = x1.shape
    ne, cap = idx.shape
    tb = _tile(t, 2 * _LANES)
    nb = t // tb
    assert cap % _ROW_CHUNK == 0
    idx_block = min(tb, cap)
    n_ib = cap // idx_block
    lo = jnp.concatenate([before[:, ::tb // _LANES], jnp.full((ne, 1), cap, _I32)], axis=1)
    idx3 = idx.reshape(ne, n_ib, idx_block)
    windows = jnp.concatenate([idx3, jnp.concatenate([idx3[:, 1:], idx3[:, -1:]], axis=1)], axis=2)
    windows = windows.reshape(ne, n_ib, 1, 2 * idx_block)

    def idx_spec(e):
        return pl.BlockSpec((1, 1, 1, 2 * idx_block),
                            lambda b, lo_ref: (e, _div(lo_ref[e, b], idx_block), 0, 0),
                            memory_space=pltpu.SMEM)

    region = tb + _ROW_CHUNK
    row = pl.BlockSpec((tb, d), lambda b, lo_ref: (b, 0))
    grid_spec = pltpu.PrefetchScalarGridSpec(
        num_scalar_prefetch=1,
        grid=(nb,),
        in_specs=[row, pl.BlockSpec(memory_space=pl.ANY)] + [idx_spec(e) for e in range(ne)],
        out_specs=row,
        scratch_shapes=[pltpu.VMEM((2, ne * region, d), _F32), pltpu.SemaphoreType.DMA((2,))])
    return pl.pallas_call(
        functools.partial(_combine_kernel, n_experts=ne, cap=cap, idx_block=idx_block),
        grid_spec=grid_spec,
        out_shape=jax.ShapeDtypeStruct((t, d), _F32),
        compiler_params=_cparams("arbitrary"),
        name="moe_combine",
    )(lo, x1, ye.reshape(ne * cap, d), *([windows] * ne))


def _final_norm_kernel(x_ref, g_ref, o_ref):
    o_ref[...] = _rmsnorm(x_ref[...], g_ref[...])


def _final_norm(x, g):
    t, d = x.shape
    tm = _tile(t, 1024)
    row = pl.BlockSpec((tm, d), lambda i: (i, 0))
    return pl.pallas_call(
        _final_norm_kernel,
        grid=(t // tm,),
        in_specs=[row, pl.BlockSpec((1, d), lambda i: (0, 0))],
        out_specs=row,
        out_shape=jax.ShapeDtypeStruct((t, d), _F32),
        compiler_params=_cparams("parallel"),
        name="final_norm",
    )(x, g.reshape(1, d))


def _moe(x1, hn, aff_t, wg, wu, wd):
    t, d = x1.shape
    ne = aff_t.shape[0]
    cap = max(1, _CAPACITY_FACTOR * t // ne)
    idx, gates, before = _select(aff_t, cap)
    ye = _expert_ffn(hn, idx, gates, wg, wu, wd)
    return _combine(x1, ye, idx, before)


def _trunk(x, p):
    batch, seq, d = x.shape
    x = x.reshape(batch * seq, d)
    depth = p["ffn_norm_g"].shape[0]
    for i in range(depth):
        j = i // _N_MIXERS
        if i % _N_MIXERS == 0:
            lambda_init = 0.8 - 0.6 * float(np.exp(-0.3 * i))
            q, k, v = _diff_qkv(x, p["diff_norm_g"][j], p["diff_w_qkv"][j], batch, seq)
            o = _diff_attn(q, k, v, p["diff_lambda_q1"][j], p["diff_lambda_k1"][j],
                           p["diff_lambda_q2"][j], p["diff_lambda_k2"][j], p["diff_subln_g"][j],
                           batch, seq, lambda_init)
            w_o = p["diff_w_o"][j]
        else:
            q, k, v = _mla_proj(x, p["mla_norm_g"][j], p["mla_w_a"][j], p["mla_q_norm_g"][j],
                                p["mla_w_q_b"][j], p["mla_kv_norm_g"][j], p["mla_w_kv_b"][j],
                                batch, seq)
            o = _mla_attn(q, k, v, batch, seq)
            w_o = p["mla_w_o"][j]
        x1, hn, aff_t = _oproj_router(o, w_o, x, p["ffn_norm_g"][i], p["router_w"][i])
        x = _moe(x1, hn, aff_t, p["w_gate_bf16"][i], p["w_up_bf16"][i], p["w_down_bf16"][i])
    return _final_norm(x, p["final_norm_g"]).reshape(batch, seq, d)


def kernel(x_prompt, x_sample, diff_norm_g, diff_w_qkv, diff_lambda_q1, diff_lambda_k1, diff_lambda_q2,
           diff_lambda_k2, diff_subln_g, diff_w_o, mla_norm_g, mla_w_a, mla_q_norm_g, mla_w_q_b,
           mla_kv_norm_g, mla_w_kv_b, mla_w_o, ffn_norm_g, router_w, w_gate, w_up, w_down, final_norm_g):
    p = dict(diff_norm_g=diff_norm_g, diff_w_qkv=diff_w_qkv, diff_lambda_q1=diff_lambda_q1,
             diff_lambda_k1=diff_lambda_k1, diff_lambda_q2=diff_lambda_q2, diff_lambda_k2=diff_lambda_k2,
             diff_subln_g=diff_subln_g, diff_w_o=diff_w_o, mla_norm_g=mla_norm_g, mla_w_a=mla_w_a,
             mla_q_norm_g=mla_q_norm_g, mla_w_q_b=mla_w_q_b, mla_kv_norm_g=mla_kv_norm_g,
             mla_w_kv_b=mla_w_kv_b, mla_w_o=mla_w_o, ffn_norm_g=ffn_norm_g, router_w=router_w,
             final_norm_g=final_norm_g, w_gate_bf16=w_gate.astype(_BF16), w_up_bf16=w_up.astype(_BF16),
             w_down_bf16=w_down.astype(_BF16))
    return _trunk(x_prompt, p), _trunk(x_sample, p)
```

```python
import functools

import numpy as np
import jax
import jax.numpy as jnp
from jax import lax
from jax.experimental import pallas as pl
from jax.experimental.pallas import tpu as pltpu

_F32, _BF16, _I32 = jnp.float32, jnp.bfloat16, jnp.int32
_EPS = 1e-6
_ROPE_THETA = 500000.0
_LANES = 128
_NEG = -1e30
_VMEM_LIMIT = 56 * 1024 * 1024

_DIFF_HEADS, _DIFF_HEAD_DIM, _DIFF_ROT = 8, 64, 16
_MLA_HEADS, _MLA_NOPE, _MLA_ROPE, _MLA_V = 16, 64, 32, 64
_MLA_Q_RANK, _MLA_KV_RANK = 384, 256
_N_EXPERTS, _CAPACITY_FACTOR = 16, 2
_N_MIXERS = 2


def _cparams(*sem):
    return pltpu.CompilerParams(dimension_semantics=sem, vmem_limit_bytes=_VMEM_LIMIT)


def _tile(n, pref):
    t = min(n, pref)
    assert n % t == 0, (n, pref)
    return t


def _rmsnorm(x, g):
    return x * lax.rsqrt(jnp.mean(x * x, axis=-1, keepdims=True) + _EPS) * g


def _rope_lanes(y, c, s_up, s_dn, half):
    return y * c + pltpu.roll(y, _LANES - half, 1) * s_up + pltpu.roll(y, half, 1) * s_dn


def _rope_tables(seq, group, start, rot):
    half = rot // 2
    pos = jnp.arange(seq, dtype=_F32)
    inv = jnp.float32(_ROPE_THETA) ** (-jnp.arange(0, rot, 2, dtype=_F32) / rot)
    ang = pos[:, None] * inv[None, :]
    cos, sin = jnp.cos(ang), jnp.sin(ang)
    j = (np.arange(_LANES) % group) - start
    first = (j >= 0) & (j < half)
    second = (j >= half) & (j < rot)
    f = np.where(first, j, np.where(second, j - half, 0))
    c = jnp.where(first | second, cos[:, f], 1.0)
    s_up = jnp.where(first, -sin[:, f], 0.0)
    s_dn = jnp.where(second, sin[:, f], 0.0)
    return c, s_up, s_dn


_LOG2E = 1.4426950408889634
_ONES_ROWS = 16
_CHUNK_UNROLL = 2


def _query_tiles(seq):
    tqs = _tile(seq, 256)
    return _tile(seq, 2 * tqs), tqs


def _flash_scratch(n, tq, tk, dv):
    return [pltpu.VMEM((n, tk, tq), _F32), pltpu.VMEM((n, tk, tq), _BF16),
            pltpu.VMEM((n, dv + _ONES_ROWS, tq), _F32)]


def _flash_multi(qs, k_refs, vt_ref, s_scr, p_scr, acc_scr):
    n, tk, tq = s_scr.shape
    n_chunks = vt_ref.shape[0]
    dv = vt_ref.shape[1] - _ONES_ROWS

    def scores(j):
        out = []
        for c in range(n):
            ks = k_refs[c][pl.ds(pl.multiple_of(j * tk, tk), tk), :]
            s = lax.dot_general(ks, qs[c], (((1,), (1,)), ((), ())), preferred_element_type=_F32)
            s_scr[c] = s
            out.append(jnp.max(s, axis=0, keepdims=True))
        return out

    def softmax(smax, st):
        out = []
        for c in range(n):
            m_new = jnp.maximum(st[c][0], smax[c])
            alpha = jnp.exp2(st[c][0] - m_new)
            p_scr[c] = jnp.exp2(s_scr[c] - m_new).astype(_BF16)
            out.append((m_new, alpha))
        return out

    def accumulate(st, j):
        for c in range(n):
            acc_scr[c] = st[c][1] * acc_scr[c] + jnp.dot(vt_ref[j], p_scr[c],
                                                         preferred_element_type=_F32)

    acc_scr[...] = jnp.zeros(acc_scr.shape, _F32)
    st = [(jnp.full((1, tq), _NEG, _F32), None)] * n
    st = softmax(scores(0), st)
    if n_chunks > 1:
        def body(j, carry):
            smax, st = carry
            accumulate(st, j - 1)
            st = softmax(smax, st)
            return scores(j + 1), st

        smax, st = lax.fori_loop(1, n_chunks - 1, body, (scores(1), st), unroll=_CHUNK_UNROLL)
        accumulate(st, n_chunks - 2)
        st = softmax(smax, st)
    accumulate(st, n_chunks - 1)
    return [(acc_scr[c, :dv, :] * (1.0 / acc_scr[c, dv:dv + 1, :])).T for c in range(n)]


_NT = (((1,), (1,)), ((), ()))


def _store_vt(vt_ref, vt, heads):
    ones = jnp.ones((_ONES_ROWS, vt.shape[1]), _BF16)
    for j in range(heads):
        vt_ref[0, j, 0, :_LANES, :] = vt[j * _LANES:(j + 1) * _LANES, :].astype(_BF16)
        vt_ref[0, j, 0, _LANES:, :] = ones


def _diff_qkv_kernel(x_ref, g_ref, w_ref, wvt_ref, c_ref, su_ref, sd_ref, q_ref, k_ref, vt_ref,
                     *, scale):
    h = _rmsnorm(x_ref[...], g_ref[...]).astype(_BF16)
    y = jnp.dot(h, w_ref[...], preferred_element_type=_F32)
    d = q_ref.shape[-1]
    c, su, sd = c_ref[...], su_ref[...], sd_ref[...]
    half = _DIFF_ROT // 2
    for j in range(d // _LANES):
        lo, hi = j * _LANES, (j + 1) * _LANES
        q_ref[:, lo:hi] = (_rope_lanes(y[:, lo:hi], c, su, sd, half) * scale).astype(_BF16)
        k_ref[:, lo:hi] = _rope_lanes(y[:, d + lo:d + hi], c, su, sd, half).astype(_BF16)
    _store_vt(vt_ref, lax.dot_general(wvt_ref[...], h, _NT, preferred_element_type=_F32), _DIFF_HEADS)


def _vt_shape_spec(batch, heads, seq, tm):
    nseq = seq // tm
    rows = _LANES + _ONES_ROWS
    return (jax.ShapeDtypeStruct((batch, heads, nseq, rows, tm), _BF16),
            pl.BlockSpec((1, heads, 1, rows, tm), lambda i: (i // nseq, 0, i % nseq, 0, 0)))


def _diff_qkv(x, g, w, batch, seq):
    t, d = x.shape
    tm = _tile(seq, 512)
    c, su, sd = _rope_tables(seq, _DIFF_HEAD_DIM, 0, _DIFF_ROT)
    nseq = seq // tm
    tab = pl.BlockSpec((tm, _LANES), lambda i: (i % nseq, 0))
    row = pl.BlockSpec((tm, d), lambda i: (i, 0))
    out = jax.ShapeDtypeStruct((t, d), _BF16)
    vt_shape, vt_spec = _vt_shape_spec(batch, _DIFF_HEADS, seq, tm)
    return pl.pallas_call(
        functools.partial(_diff_qkv_kernel, scale=_DIFF_HEAD_DIM ** -0.5 * _LOG2E),
        grid=(t // tm,),
        in_specs=[row, pl.BlockSpec((1, d), lambda i: (0, 0)),
                  pl.BlockSpec((d, 2 * d), lambda i: (0, 0)),
                  pl.BlockSpec((d, d), lambda i: (0, 0)), tab, tab, tab],
        out_specs=[row, row, vt_spec],
        out_shape=[out, out, vt_shape],
        compiler_params=_cparams("parallel"),
        name="diff_qkv",
    )(x, g.reshape(1, d), w[:, :2 * d].astype(_BF16), w[:, 2 * d:].T.astype(_BF16), c, su, sd)


def _diff_attn_kernel(q_ref, k_ref, v_ref, lq1_ref, lk1_ref, lq2_ref, lk2_ref, g_ref, o_ref,
                      s_scr, p_scr, acc_scr, *, lambda_init):
    tqs = s_scr.shape[-1]
    qs = []
    for r in range(q_ref.shape[0] // tqs):
        q = q_ref[r * tqs:(r + 1) * tqs, :]
        lane = lax.broadcasted_iota(_I32, q.shape, 1)
        zero = jnp.zeros_like(q)
        qs += [jnp.where(lane < _DIFF_HEAD_DIM, q, zero), jnp.where(lane >= _DIFF_HEAD_DIM, q, zero)]
    outs = _flash_multi(qs, [k_ref] * len(qs), v_ref, s_scr, p_scr, acc_scr)
    lam = (jnp.exp(jnp.sum(lq1_ref[...] * lk1_ref[...], axis=-1, keepdims=True))
           - jnp.exp(jnp.sum(lq2_ref[...] * lk2_ref[...], axis=-1, keepdims=True)) + lambda_init)
    for r in range(len(qs) // 2):
        o = outs[2 * r] - lam * outs[2 * r + 1]
        o = _rmsnorm(o, g_ref[...]) * (1.0 - lambda_init)
        o_ref[r * tqs:(r + 1) * tqs, :] = o.astype(_BF16)


def _vt_block(vt):
    return pl.BlockSpec((None, None) + vt.shape[2:], lambda b, h, i: (b, h, 0, 0, 0))


def _diff_attn(q, k, vt, lq1, lk1, lq2, lk2, subln_g, batch, seq, lambda_init):
    t, d = q.shape
    tq, tqs = _query_tiles(seq)
    tk = vt.shape[-1]
    nq = seq // tq
    qspec = pl.BlockSpec((tq, _LANES), lambda b, h, i: (b * nq + i, h))
    kspec = pl.BlockSpec((seq, _LANES), lambda b, h, i: (b, h))
    small = lambda n: pl.BlockSpec((1, n), lambda b, h, i: (0, 0))
    hd = _DIFF_HEAD_DIM
    return pl.pallas_call(
        functools.partial(_diff_attn_kernel, lambda_init=lambda_init),
        grid=(batch, _DIFF_HEADS, nq),
        in_specs=[qspec, kspec, _vt_block(vt), small(hd), small(hd), small(hd), small(hd),
                  small(2 * hd)],
        out_specs=qspec,
        out_shape=jax.ShapeDtypeStruct((t, d), _BF16),
        scratch_shapes=_flash_scratch(2 * (tq // tqs), tqs, tk, _LANES),
        compiler_params=_cparams("parallel", "parallel", "arbitrary"),
        name="diff_attn",
    )(q, k, vt, lq1.reshape(1, hd), lk1.reshape(1, hd), lq2.reshape(1, hd), lk2.reshape(1, hd),
      subln_g.reshape(1, 2 * hd))


def _mla_proj_kernel(x_ref, g_ref, wa_ref, gq_ref, wq_ref, gkv_ref, wk_ref, wvt_ref, c_ref, su_ref,
                     sd_ref, q_ref, k_ref, vt_ref, *, scale):
    h = _rmsnorm(x_ref[...], g_ref[...]).astype(_BF16)
    a = jnp.dot(h, wa_ref[...], preferred_element_type=_F32)
    c, su, sd = c_ref[...], su_ref[...], sd_ref[...]
    half = _MLA_ROPE // 2
    kv_lo = _MLA_Q_RANK + _MLA_KV_RANK
    cq = _rmsnorm(a[:, :_MLA_Q_RANK], gq_ref[...]).astype(_BF16)
    ckv = _rmsnorm(a[:, _MLA_Q_RANK:kv_lo], gkv_ref[...]).astype(_BF16)
    k_rope = _rope_lanes(a[:, kv_lo:kv_lo + _LANES], c, su, sd, half)
    qf = jnp.dot(cq, wq_ref[...], preferred_element_type=_F32)
    kf = jnp.dot(ckv, wk_ref[...], preferred_element_type=_F32)
    for j in range(_MLA_HEADS):
        lo, hi = j * _LANES, (j + 1) * _LANES
        q_ref[:, lo:hi] = (_rope_lanes(qf[:, lo:hi], c, su, sd, half) * scale).astype(_BF16)
        k_ref[:, lo:hi] = (kf[:, lo:hi] + k_rope).astype(_BF16)
    _store_vt(vt_ref, lax.dot_general(wvt_ref[...], ckv, _NT, preferred_element_type=_F32),
              _MLA_HEADS * _MLA_V // _LANES)


def _mla_weights(w_a, w_q_b, w_kv_b):
    d = w_a.shape[0]
    kv_lo = _MLA_Q_RANK + _MLA_KV_RANK
    z = lambda *s: jnp.zeros(s, _F32)
    pad = _LANES - _MLA_NOPE - _MLA_ROPE
    wa = jnp.concatenate([w_a[:, :kv_lo], z(d, _MLA_NOPE), w_a[:, kv_lo:], z(d, pad)], axis=1)
    wq = w_q_b.reshape(_MLA_Q_RANK, _MLA_HEADS, _MLA_NOPE + _MLA_ROPE)
    wq = jnp.concatenate([wq, z(_MLA_Q_RANK, _MLA_HEADS, pad)], axis=2)
    wkv = w_kv_b.reshape(_MLA_KV_RANK, _MLA_HEADS, _MLA_NOPE + _MLA_V)
    wk = jnp.concatenate([wkv[:, :, :_MLA_NOPE], z(_MLA_KV_RANK, _MLA_HEADS, _LANES - _MLA_NOPE)], axis=2)
    wvt = wkv[:, :, _MLA_NOPE:].reshape(_MLA_KV_RANK, -1).T
    return (wa.astype(_BF16), wq.reshape(_MLA_Q_RANK, -1).astype(_BF16),
            wk.reshape(_MLA_KV_RANK, -1).astype(_BF16), wvt.astype(_BF16))


def _mla_proj(x, g, w_a, gq, w_q_b, gkv, w_kv_b, batch, seq):
    t, d = x.shape
    tm = _tile(seq, 512)
    nseq = seq // tm
    c, su, sd = _rope_tables(seq, _LANES, _MLA_NOPE, _MLA_ROPE)
    wa, wq, wk, wvt = _mla_weights(w_a, w_q_b, w_kv_b)
    hq = _MLA_HEADS * _LANES
    tab = pl.BlockSpec((tm, _LANES), lambda i: (i % nseq, 0))
    full = lambda a: pl.BlockSpec(a.shape, lambda i: (0, 0))
    row = lambda n: pl.BlockSpec((tm, n), lambda i: (i, 0))
    g, gq, gkv = g.reshape(1, -1), gq.reshape(1, -1), gkv.reshape(1, -1)
    scale = (_MLA_NOPE + _MLA_ROPE) ** -0.5 * _LOG2E
    vt_shape, vt_spec = _vt_shape_spec(batch, _MLA_HEADS * _MLA_V // _LANES, seq, tm)
    return pl.pallas_call(
        functools.partial(_mla_proj_kernel, scale=scale),
        grid=(t // tm,),
        in_specs=[row(d), full(g), full(wa), full(gq), full(wq), full(gkv), full(wk), full(wvt),
                  tab, tab, tab],
        out_specs=[row(hq), row(hq), vt_spec],
        out_shape=[jax.ShapeDtypeStruct((t, hq), _BF16), jax.ShapeDtypeStruct((t, hq), _BF16), vt_shape],
        compiler_params=_cparams("parallel"),
        name="mla_proj",
    )(x, g, wa, gq, wq, gkv, wk, wvt, c, su, sd)


def _mla_attn_kernel(q_ref, ka_ref, kb_ref, v_ref, o_ref, s_scr, p_scr, acc_scr):
    tqs = s_scr.shape[-1]
    n_rows = q_ref.shape[0] // tqs
    qs = []
    for r in range(n_rows):
        qs += [q_ref[r * tqs:(r + 1) * tqs, :_LANES], q_ref[r * tqs:(r + 1) * tqs, _LANES:]]
    outs = _flash_multi(qs, [ka_ref, kb_ref] * n_rows, v_ref, s_scr, p_scr, acc_scr)
    lane = lax.broadcasted_iota(_I32, outs[0].shape, 1)
    for r in range(n_rows):
        o_ref[r * tqs:(r + 1) * tqs, :] = jnp.where(lane < _MLA_V, outs[2 * r],
                                                    outs[2 * r + 1]).astype(_BF16)


def _mla_attn(q, k, vt, batch, seq):
    t = q.shape[0]
    tq, tqs = _query_tiles(seq)
    tk = vt.shape[-1]
    nq = seq // tq
    return pl.pallas_call(
        _mla_attn_kernel,
        grid=(batch, _MLA_HEADS // 2, nq),
        in_specs=[pl.BlockSpec((tq, 2 * _LANES), lambda b, h, i: (b * nq + i, h)),
                  pl.BlockSpec((seq, _LANES), lambda b, h, i: (b, 2 * h)),
                  pl.BlockSpec((seq, _LANES), lambda b, h, i: (b, 2 * h + 1)),
                  _vt_block(vt)],
        out_specs=pl.BlockSpec((tq, _LANES), lambda b, h, i: (b * nq + i, h)),
        out_shape=jax.ShapeDtypeStruct((t, _MLA_HEADS * _MLA_V), _BF16),
        scratch_shapes=_flash_scratch(2 * (tq // tqs), tqs, tk, _LANES),
        compiler_params=_cparams("parallel", "parallel", "arbitrary"),
        name="mla_attn",
    )(q, k, k, vt)


def _oproj_router_kernel(o_ref, w_ref, x_ref, g_ref, rwh_ref, rwl_ref, x1_ref, hn_ref, aff_ref):
    x1 = x_ref[...] + jnp.dot(o_ref[...], w_ref[...], preferred_element_type=_F32)
    x1_ref[...] = x1
    hn = _rmsnorm(x1, g_ref[...])
    _to_row_tiles(hn_ref, hn)
    hh = hn.astype(_BF16)
    hl = (hn - hh.astype(_F32)).astype(_BF16)
    nt = (((1,), (1,)), ((), ()))
    rwh = rwh_ref[...]
    logits = (lax.dot_general(rwh, hh, nt, preferred_element_type=_F32)
              + lax.dot_general(rwh, hl, nt, preferred_element_type=_F32)
              + lax.dot_general(rwl_ref[...], hh, nt, preferred_element_type=_F32))
    e = jnp.exp(logits - jnp.max(logits, axis=0, keepdims=True))
    aff_ref[...] = e / jnp.sum(e, axis=0, keepdims=True)


def _oproj_router(o, w_o, x, g, router_w):
    t, d = x.shape
    tm = _tile(t, 512)
    ne = router_w.shape[1]
    rwt = router_w.T
    rwh = rwt.astype(_BF16)
    rwl = (rwt - rwh.astype(_F32)).astype(_BF16)
    row = pl.BlockSpec((tm, d), lambda i: (i, 0))
    full = lambda a: pl.BlockSpec(a.shape, lambda i: (0, 0))
    g = g.reshape(1, d)
    w = w_o.astype(_BF16)
    return pl.pallas_call(
        _oproj_router_kernel,
        grid=(t // tm,),
        in_specs=[row, full(w), row, full(g), full(rwh), full(rwl)],
        out_specs=[row, pl.BlockSpec((tm * _SUBLANES, _LANES), lambda i: (i, 0)),
                   pl.BlockSpec((ne, tm), lambda i: (0, i))],
        out_shape=[jax.ShapeDtypeStruct((t, d), _F32),
                   jax.ShapeDtypeStruct((t * d // _LANES, _LANES), _F32),
                   jax.ShapeDtypeStruct((ne, t), _F32)],
        compiler_params=_cparams("parallel"),
        name="oproj_router",
    )(o, w, x, g, rwh, rwl)


def _select_kernel(aff_ref, idx_ref, gate_ref, base_ref, *, cap, n_experts, ns):
    rows = aff_ref.shape[0] // n_experts

    def iota(shape, dim):
        return lax.broadcasted_iota(_I32, shape, dim)

    sq = (_LANES, _LANES)
    upto = (iota(sq, 0) <= iota(sq, 1)).astype(_BF16)
    upto_t = (iota(sq, 1) <= iota(sq, 0)).astype(_BF16)
    above = (iota((rows, rows), 1) < iota((rows, rows), 0)).astype(_BF16)
    capf = jnp.float32(cap)
    row_id = iota((rows, ns), 0).astype(_F32)
    lane_id = iota((_LANES, ns), 0).astype(_F32)
    slot_id = iota((1, ns), 1).astype(_F32)

    def lanes(col):
        return jnp.broadcast_to(col, (rows, _LANES))

    def rows_before(tot):
        return jnp.dot(above, lanes(tot).astype(_BF16), preferred_element_type=_F32)[:, :1]

    def expert(e, carry):
        r0 = pl.multiple_of(e * rows, rows)
        aff = aff_ref[pl.ds(r0, rows), :]
        bits = pltpu.bitcast(aff, _I32)

        def step(i, thr):
            cand = thr | lax.shift_left(jnp.int32(1), jnp.int32(30) - lax.convert_element_type(i, _I32))
            return jnp.where(jnp.sum(jnp.where(bits >= cand, 1.0, 0.0), keepdims=True) >= capf, cand, thr)

        thr = lax.fori_loop(0, 31, step, jnp.zeros((1, 1), _I32))
        gt = bits > thr
        eq = jnp.where(bits == thr, 1.0, 0.0)
        need = capf - jnp.sum(jnp.where(gt, 1.0, 0.0), keepdims=True)
        rank = (rows_before(jnp.sum(eq, axis=-1, keepdims=True))
                + jnp.dot(eq.astype(_BF16), upto, preferred_element_type=_F32) - eq)
        sel = jnp.where(gt | ((eq > 0.0) & (rank < need)), 1.0, 0.0)

        tot = jnp.sum(sel, axis=-1, keepdims=True)
        base = rows_before(tot)
        cum = base + tot
        base_ref[pl.ds(r0, rows), :] = lanes(base).astype(_I32)
        incl_t = jnp.dot(upto_t, sel.T.astype(_BF16), preferred_element_type=_F32).astype(_BF16)
        aff_t = aff.T
        a0 = aff_t.astype(_BF16)
        a1 = (aff_t - a0.astype(_F32)).astype(_BF16)
        a2 = (aff_t - a0.astype(_F32) - a1.astype(_F32)).astype(_BF16)
        for c in range(cap // ns):
            j = slot_id + float(c * ns)
            rj = jnp.sum(jnp.where(cum <= j, 1.0, 0.0), axis=0, keepdims=True)
            hit = row_id == rj
            q = j - jnp.sum(jnp.where(hit, base, 0.0), axis=0, keepdims=True)
            onehot = jnp.where(hit, 1.0, 0.0).astype(_BF16)
            counts = jnp.dot(incl_t, onehot, preferred_element_type=_F32)
            lpos = jnp.sum(jnp.where(counts <= q, 1.0, 0.0), axis=0, keepdims=True)
            arow = (jnp.dot(a0, onehot, preferred_element_type=_F32)
                    + jnp.dot(a1, onehot, preferred_element_type=_F32)
                    + jnp.dot(a2, onehot, preferred_element_type=_F32))
            gate = jnp.sum(jnp.where(lane_id == lpos, arow, 0.0), axis=0, keepdims=True)
            idx_ref[e, pl.ds(c, 1), :] = (rj * float(_LANES) + lpos).astype(_I32)
            gate_ref[e, pl.ds(c, 1), :] = gate
        return carry

    lax.fori_loop(0, n_experts, expert, 0)


def _select(aff_t, cap):
    ne, t = aff_t.shape
    rows = t // _LANES
    ns = _tile(cap, 512)
    aff2 = aff_t.reshape(ne * rows, _LANES)
    listing = jax.ShapeDtypeStruct((ne, cap // ns, ns), _I32)
    idx, gates, base = pl.pallas_call(
        functools.partial(_select_kernel, cap=cap, n_experts=ne, ns=ns),
        out_shape=[listing, jax.ShapeDtypeStruct(listing.shape, _F32),
                   jax.ShapeDtypeStruct(aff2.shape, _I32)],
        compiler_params=pltpu.CompilerParams(vmem_limit_bytes=_VMEM_LIMIT),
        name="expert_select",
    )(aff2)
    return idx.reshape(ne, cap), gates.reshape(ne, cap), base.reshape(ne, rows, _LANES)[:, :, 0]


_SUBLANES = 8


def _to_row_tiles(dst_ref, x):
    n = x.shape[0]
    for j in range(x.shape[1] // _LANES):
        dst_ref[pl.ds(j, n, stride=_SUBLANES), :] = x[:, j * _LANES:(j + 1) * _LANES]


def _from_row_tiles(src_ref, n):
    return jnp.concatenate([src_ref[pl.ds(j, n, stride=_SUBLANES), :] for j in range(_SUBLANES)],
                           axis=1)


def _ffn_kernel(idx_ref, nxt_ref, hn_hbm, gate_ref, wg_ref, wu_ref, wd_ref, ye_ref, xbuf, sems,
                *, f_chunk):
    tc = xbuf.shape[1] // _SUBLANES
    n_tiles = pl.num_programs(1)
    step = pl.program_id(0) * n_tiles + pl.program_id(1)
    last = pl.num_programs(0) * n_tiles - 1
    slot = step % 2

    def row_copy(j, t, buf):
        src = hn_hbm.at[pl.ds(pl.multiple_of(t * _SUBLANES, _SUBLANES), _SUBLANES)]
        dst = xbuf.at[buf, pl.ds(pl.multiple_of(j * _SUBLANES, _SUBLANES), _SUBLANES)]
        return pltpu.make_async_copy(src, dst, sems.at[buf])

    def gather(ids_ref, buf, unroll):
        def issue(j, carry):
            row_copy(j, ids_ref[0, 0, j], buf).start()
            return carry
        lax.fori_loop(0, tc, issue, 0, unroll=unroll)

    @pl.when(step == 0)
    def _():
        gather(idx_ref, 0, 8)

    @pl.when(step < last)
    def _():
        gather(nxt_ref, 1 - slot, True)

    def drain(j, carry):
        row_copy(j, 0, slot).wait()
        return carry

    lax.fori_loop(0, tc, drain, 0, unroll=True)
    x = _from_row_tiles(xbuf.at[slot], tc).astype(_BF16)
    d_ff = wg_ref.shape[-1]
    y = jnp.zeros((tc, wd_ref.shape[-1]), _F32)
    for f in range(0, d_ff, f_chunk):
        g = jnp.dot(x, wg_ref[0, :, f:f + f_chunk], preferred_element_type=_F32)
        u = jnp.dot(x, wu_ref[0, :, f:f + f_chunk], preferred_element_type=_F32)
        hid = (g * jax.nn.sigmoid(g) * u).astype(_BF16)
        y = y + jnp.dot(hid, wd_ref[0, f:f + f_chunk, :], preferred_element_type=_F32)
    _to_row_tiles(ye_ref, y * gate_ref[0])


def _expert_ffn(hn, idx, gates, wg, wu, wd, layer):
    ne, cap = idx.shape
    d, d_ff = wg.shape[-2:]
    tc = _tile(cap, 512)
    nc = cap // tc
    last = ne * nc - 1
    ids = idx.reshape(ne * nc, 1, tc)
    return pl.pallas_call(
        functools.partial(_ffn_kernel, f_chunk=_tile(d_ff, 512)),
        grid=(ne, nc),
        in_specs=[pl.BlockSpec((1, 1, tc), lambda e, j: (e * nc + j, 0, 0), memory_space=pltpu.SMEM),
                  pl.BlockSpec((1, 1, tc), lambda e, j: (jnp.minimum(e * nc + j + 1, last), 0, 0),
                               memory_space=pltpu.SMEM),
                  pl.BlockSpec(memory_space=pl.ANY),
                  pl.BlockSpec((1, tc, 1), lambda e, j: (e, j, 0)),
                  pl.BlockSpec((None, 1, d, d_ff), lambda e, j: (layer, e, 0, 0)),
                  pl.BlockSpec((None, 1, d, d_ff), lambda e, j: (layer, e, 0, 0)),
                  pl.BlockSpec((None, 1, d_ff, d), lambda e, j: (layer, e, 0, 0))],
        out_specs=pl.BlockSpec((tc * _SUBLANES, _LANES), lambda e, j: (e * nc + j, 0)),
        out_shape=jax.ShapeDtypeStruct((ne * cap * d // _LANES, _LANES), _F32),
        scratch_shapes=[pltpu.VMEM((2, tc * _SUBLANES, _LANES), _F32), pltpu.SemaphoreType.DMA((2,))],
        compiler_params=_cparams("arbitrary", "arbitrary"),
        name="expert_ffn",
    )(ids, ids, hn, gates.reshape(ne, cap, 1), wg, wu, wd)


_ROW_CHUNK = 8
_RMW_GROUP = 8


def _div(x, n):
    if n & (n - 1) == 0:
        return lax.shift_right_logical(x, jnp.int32(n.bit_length() - 1))
    return lax.div(x, jnp.int32(n))


def _combine_kernel(lo_ref, x_ref, ye_hbm, *rest, n_experts, cap, idx_block, final_norm):
    idx_refs, rest = rest[:n_experts], rest[n_experts:]
    if final_norm:
        g_ref, rest = rest[0], rest[1:]
    out_ref, acc, stage, sems = rest
    tb = x_ref.shape[0]
    region = stage.shape[1] // (n_experts * _SUBLANES)
    b = pl.program_id(0)
    slot = b % 2

    def tile_of(row):
        return pl.ds(pl.multiple_of(row * _SUBLANES, _SUBLANES), _SUBLANES)

    def for_each_chunk(blk, buf, fn):
        size = _ROW_CHUNK * _SUBLANES
        for e in range(n_experts):
            lo, hi = lo_ref[e, blk], lo_ref[e, blk + 1]
            c0 = _div(lo, _ROW_CHUNK)

            def one(c, carry):
                src = pl.multiple_of((e * cap + c * _ROW_CHUNK) * _SUBLANES, size)
                dst = pl.multiple_of((e * region + (c - c0) * _ROW_CHUNK) * _SUBLANES, size)
                fn(pltpu.make_async_copy(ye_hbm.at[pl.ds(src, size)],
                                         stage.at[buf, pl.ds(dst, size)], sems.at[buf]))
                return carry

            lax.fori_loop(c0, _div(hi + (_ROW_CHUNK - 1), _ROW_CHUNK), one, 0)

    @pl.when(b == 0)
    def _():
        for_each_chunk(0, 0, lambda cp: cp.start())

    @pl.when(b + 1 < pl.num_programs(0))
    def _():
        for_each_chunk(b + 1, 1 - slot, lambda cp: cp.start())

    _to_row_tiles(acc, x_ref[...])
    for_each_chunk(b, slot, lambda cp: cp.wait())
    for e in range(n_experts):
        lo, hi = lo_ref[e, b], lo_ref[e, b + 1]
        first = _div(lo, idx_block) * idx_block
        row0 = e * region - _div(lo, _ROW_CHUNK) * _ROW_CHUNK
        ids = idx_refs[e]

        def token(j):
            return ids[0, 0, 0, j - first] - b * tb

        def add_rows(j0, count):
            toks = [tile_of(token(j0 + i)) for i in range(count)]
            sums = [acc[toks[i], :] + stage[slot, tile_of(row0 + j0 + i), :] for i in range(count)]
            for i in range(count):
                acc[toks[i], :] = sums[i]

        def group(g, carry):
            add_rows(lo + g * _RMW_GROUP, _RMW_GROUP)
            return carry

        def single(j, carry):
            add_rows(j, 1)
            return carry

        n_groups = _div(hi - lo, _RMW_GROUP)
        lax.fori_loop(0, n_groups, group, 0)
        lax.fori_loop(lo + n_groups * _RMW_GROUP, hi, single, 0)

    y = _from_row_tiles(acc, tb)
    out_ref[...] = _rmsnorm(y, g_ref[...]) if final_norm else y


def _combine(x1, ye, idx, before, final_g=None):
    t, d = x1.shape
    ne, cap = idx.shape
    assert d == _SUBLANES * _LANES
    tb = _tile(t, 2 * _LANES)
    nb = t // tb
    assert cap % _ROW_CHUNK == 0
    idx_block = min(tb, cap)
    n_ib = cap // idx_block
    lo = jnp.concatenate([before[:, ::tb // _LANES], jnp.full((ne, 1), cap, _I32)], axis=1)
    idx3 = idx.reshape(ne, n_ib, idx_block)
    windows = jnp.concatenate([idx3, jnp.concatenate([idx3[:, 1:], idx3[:, -1:]], axis=1)], axis=2)
    windows = windows.reshape(ne, n_ib, 1, 2 * idx_block)

    def idx_spec(e):
        return pl.BlockSpec((1, 1, 1, 2 * idx_block),
                            lambda b, lo_ref: (e, _div(lo_ref[e, b], idx_block), 0, 0),
                            memory_space=pltpu.SMEM)

    region = tb + _ROW_CHUNK
    row = pl.BlockSpec((tb, d), lambda b, lo_ref: (b, 0))
    norm_specs, norm_args = [], []
    if final_g is not None:
        norm_specs, norm_args = [pl.BlockSpec((1, d), lambda b, lo_ref: (0, 0))], [final_g.reshape(1, d)]
    grid_spec = pltpu.PrefetchScalarGridSpec(
        num_scalar_prefetch=1,
        grid=(nb,),
        in_specs=[row, pl.BlockSpec(memory_space=pl.ANY)] + [idx_spec(e) for e in range(ne)]
                 + norm_specs,
        out_specs=row,
        scratch_shapes=[pltpu.VMEM((tb * _SUBLANES, _LANES), _F32),
                        pltpu.VMEM((2, ne * region * _SUBLANES, _LANES), _F32),
                        pltpu.SemaphoreType.DMA((2,))])
    return pl.pallas_call(
        functools.partial(_combine_kernel, n_experts=ne, cap=cap, idx_block=idx_block,
                          final_norm=final_g is not None),
        grid_spec=grid_spec,
        out_shape=jax.ShapeDtypeStruct((t, d), _F32),
        compiler_params=_cparams("arbitrary"),
        name="moe_combine",
    )(lo, x1, ye, *([windows] * ne), *norm_args)


def _moe(x1, hn, aff_t, wg, wu, wd, layer, final_g):
    t, d = x1.shape
    ne = aff_t.shape[0]
    cap = max(1, _CAPACITY_FACTOR * t // ne)
    idx, gates, before = _select(aff_t, cap)
    ye = _expert_ffn(hn, idx, gates, wg, wu, wd, layer)
    return _combine(x1, ye, idx, before, final_g)


def _trunk(x, p):
    batch, seq, d = x.shape
    x = x.reshape(batch * seq, d)
    depth = p["ffn_norm_g"].shape[0]
    for i in range(depth):
        j = i // _N_MIXERS
        if i % _N_MIXERS == 0:
            lambda_init = 0.8 - 0.6 * float(np.exp(-0.3 * i))
            q, k, v = _diff_qkv(x, p["diff_norm_g"][j], p["diff_w_qkv"][j], batch, seq)
            o = _diff_attn(q, k, v, p["diff_lambda_q1"][j], p["diff_lambda_k1"][j],
                           p["diff_lambda_q2"][j], p["diff_lambda_k2"][j], p["diff_subln_g"][j],
                           batch, seq, lambda_init)
            w_o = p["diff_w_o"][j]
        else:
            q, k, v = _mla_proj(x, p["mla_norm_g"][j], p["mla_w_a"][j], p["mla_q_norm_g"][j],
                                p["mla_w_q_b"][j], p["mla_kv_norm_g"][j], p["mla_w_kv_b"][j],
                                batch, seq)
            o = _mla_attn(q, k, v, batch, seq)
            w_o = p["mla_w_o"][j]
        x1, hn, aff_t = _oproj_router(o, w_o, x, p["ffn_norm_g"][i], p["router_w"][i])
        x = _moe(x1, hn, aff_t, p["w_gate_bf16"], p["w_up_bf16"], p["w_down_bf16"], i,
                 p["final_norm_g"] if i == depth - 1 else None)
    return x.reshape(batch, seq, d)


def kernel(x_prompt, x_sample, diff_norm_g, diff_w_qkv, diff_lambda_q1, diff_lambda_k1, diff_lambda_q2,
           diff_lambda_k2, diff_subln_g, diff_w_o, mla_norm_g, mla_w_a, mla_q_norm_g, mla_w_q_b,
           mla_kv_norm_g, mla_w_kv_b, mla_w_o, ffn_norm_g, router_w, w_gate, w_up, w_down, final_norm_g):
    p = dict(diff_norm_g=diff_norm_g, diff_w_qkv=diff_w_qkv, diff_lambda_q1=diff_lambda_q1,
             diff_lambda_k1=diff_lambda_k1, diff_lambda_q2=diff_lambda_q2, diff_lambda_k2=diff_lambda_k2,
             diff_subln_g=diff_subln_g, diff_w_o=diff_w_o, mla_norm_g=mla_norm_g, mla_w_a=mla_w_a,
             mla_q_norm_g=mla_q_norm_g, mla_w_q_b=mla_w_q_b, mla_kv_norm_g=mla_kv_norm_g,
             mla_w_kv_b=mla_w_kv_b, mla_w_o=mla_w_o, ffn_norm_g=ffn_norm_g, router_w=router_w,
             final_norm_g=final_norm_g, w_gate_bf16=w_gate.astype(_BF16), w_up_bf16=w_up.astype(_BF16),
             w_down_bf16=w_down.astype(_BF16))
    return _trunk(x_prompt, p), _trunk(x_sample, p)
```

```python
import functools

import numpy as np
import jax
import jax.numpy as jnp
from jax import lax
from jax.experimental import pallas as pl
from jax.experimental.pallas import tpu as pltpu

_F32, _BF16, _I32 = jnp.float32, jnp.bfloat16, jnp.int32
_EPS = 1e-6
_ROPE_THETA = 500000.0
_LANES = 128
_NEG = -1e30
_VMEM_LIMIT = 56 * 1024 * 1024

_DIFF_HEADS, _DIFF_HEAD_DIM, _DIFF_ROT = 8, 64, 16
_MLA_HEADS, _MLA_NOPE, _MLA_ROPE, _MLA_V = 16, 64, 32, 64
_MLA_Q_RANK, _MLA_KV_RANK = 384, 256
_N_EXPERTS, _CAPACITY_FACTOR = 16, 2
_N_MIXERS = 2


def _cparams(*sem):
    return pltpu.CompilerParams(dimension_semantics=sem, vmem_limit_bytes=_VMEM_LIMIT)


def _tile(n, pref):
    t = min(n, pref)
    assert n % t == 0, (n, pref)
    return t


def _rmsnorm(x, g):
    return x * lax.rsqrt(jnp.mean(x * x, axis=-1, keepdims=True) + _EPS) * g


def _rope_lanes(y, c, s_up, s_dn, half):
    return y * c + pltpu.roll(y, _LANES - half, 1) * s_up + pltpu.roll(y, half, 1) * s_dn


def _rope_tables(seq, group, start, rot):
    half = rot // 2
    pos = jnp.arange(seq, dtype=_F32)
    inv = jnp.float32(_ROPE_THETA) ** (-jnp.arange(0, rot, 2, dtype=_F32) / rot)
    ang = pos[:, None] * inv[None, :]
    cos, sin = jnp.cos(ang), jnp.sin(ang)
    j = (np.arange(_LANES) % group) - start
    first = (j >= 0) & (j < half)
    second = (j >= half) & (j < rot)
    f = np.where(first, j, np.where(second, j - half, 0))
    c = jnp.where(first | second, cos[:, f], 1.0)
    s_up = jnp.where(first, -sin[:, f], 0.0)
    s_dn = jnp.where(second, sin[:, f], 0.0)
    return c, s_up, s_dn


_LOG2E = 1.4426950408889634
_ONES_ROWS = 16
_CHUNK_UNROLL = 2


def _key_chunk(seq):
    return _tile(seq, 512)


def _query_tiles(seq):
    tqs = _tile(seq, 256)
    return _tile(seq, 2 * tqs), tqs


def _flash_scratch(n, tq, tk, dv):
    return [pltpu.VMEM((n, tk, tq), _F32), pltpu.VMEM((n, tk, tq), _BF16),
            pltpu.VMEM((n, dv + _ONES_ROWS, tq), _F32)]


def _flash_multi(qs, k_refs, vt_ref, s_scr, p_scr, acc_scr):
    n, tk, tq = s_scr.shape
    n_chunks = vt_ref.shape[0]
    dv = vt_ref.shape[1] - _ONES_ROWS

    def scores(j):
        out = []
        for c in range(n):
            ks = k_refs[c][pl.ds(pl.multiple_of(j * tk, tk), tk), :]
            s = lax.dot_general(ks, qs[c], (((1,), (1,)), ((), ())), preferred_element_type=_F32)
            s_scr[c] = s
            out.append(jnp.max(s, axis=0, keepdims=True))
        return out

    def softmax(smax, st):
        out = []
        for c in range(n):
            m_new = jnp.maximum(st[c][0], smax[c])
            alpha = jnp.exp2(st[c][0] - m_new)
            p_scr[c] = jnp.exp2(s_scr[c] - m_new).astype(_BF16)
            out.append((m_new, alpha))
        return out

    def accumulate(st, j):
        for c in range(n):
            acc_scr[c] = st[c][1] * acc_scr[c] + jnp.dot(vt_ref[j], p_scr[c],
                                                         preferred_element_type=_F32)

    acc_scr[...] = jnp.zeros(acc_scr.shape, _F32)
    st = [(jnp.full((1, tq), _NEG, _F32), None)] * n
    st = softmax(scores(0), st)
    if n_chunks > 1:
        def body(j, carry):
            smax, st = carry
            accumulate(st, j - 1)
            st = softmax(smax, st)
            return scores(j + 1), st

        unroll = _CHUNK_UNROLL if n_chunks - 2 >= 2 * _CHUNK_UNROLL else 1
        smax, st = lax.fori_loop(1, n_chunks - 1, body, (scores(1), st), unroll=unroll)
        accumulate(st, n_chunks - 2)
        st = softmax(smax, st)
    accumulate(st, n_chunks - 1)
    return [(acc_scr[c, :dv, :] * (1.0 / acc_scr[c, dv:dv + 1, :])).T for c in range(n)]


_NT = (((1,), (1,)), ((), ()))


def _store_vt(vt_ref, vt, heads):
    ones = jnp.ones((_ONES_ROWS, vt.shape[1]), _BF16)
    for j in range(heads):
        vt_ref[0, j, 0, :_LANES, :] = vt[j * _LANES:(j + 1) * _LANES, :].astype(_BF16)
        vt_ref[0, j, 0, _LANES:, :] = ones


def _diff_qkv_kernel(x_ref, g_ref, w_ref, wvt_ref, c_ref, su_ref, sd_ref, q_ref, k_ref, vt_ref,
                     *, scale):
    h = _rmsnorm(x_ref[...], g_ref[...]).astype(_BF16)
    y = jnp.dot(h, w_ref[...], preferred_element_type=_F32)
    d = q_ref.shape[-1]
    c, su, sd = c_ref[...], su_ref[...], sd_ref[...]
    half = _DIFF_ROT // 2
    for j in range(d // _LANES):
        lo, hi = j * _LANES, (j + 1) * _LANES
        q_ref[:, lo:hi] = (_rope_lanes(y[:, lo:hi], c, su, sd, half) * scale).astype(_BF16)
        k_ref[:, lo:hi] = _rope_lanes(y[:, d + lo:d + hi], c, su, sd, half).astype(_BF16)
    _store_vt(vt_ref, lax.dot_general(wvt_ref[...], h, _NT, preferred_element_type=_F32), _DIFF_HEADS)


def _vt_shape_spec(batch, heads, seq, tm):
    nseq = seq // tm
    rows = _LANES + _ONES_ROWS
    return (jax.ShapeDtypeStruct((batch, heads, nseq, rows, tm), _BF16),
            pl.BlockSpec((1, heads, 1, rows, tm), lambda i: (i // nseq, 0, i % nseq, 0, 0)))


def _diff_qkv(x, g, w, batch, seq):
    t, d = x.shape
    tm = _key_chunk(seq)
    c, su, sd = _rope_tables(seq, _DIFF_HEAD_DIM, 0, _DIFF_ROT)
    nseq = seq // tm
    tab = pl.BlockSpec((tm, _LANES), lambda i: (i % nseq, 0))
    row = pl.BlockSpec((tm, d), lambda i: (i, 0))
    out = jax.ShapeDtypeStruct((t, d), _BF16)
    vt_shape, vt_spec = _vt_shape_spec(batch, _DIFF_HEADS, seq, tm)
    return pl.pallas_call(
        functools.partial(_diff_qkv_kernel, scale=_DIFF_HEAD_DIM ** -0.5 * _LOG2E),
        grid=(t // tm,),
        in_specs=[row, pl.BlockSpec((1, d), lambda i: (0, 0)),
                  pl.BlockSpec((d, 2 * d), lambda i: (0, 0)),
                  pl.BlockSpec((d, d), lambda i: (0, 0)), tab, tab, tab],
        out_specs=[row, row, vt_spec],
        out_shape=[out, out, vt_shape],
        compiler_params=_cparams("parallel"),
        name="diff_qkv",
    )(x, g.reshape(1, d), w[:, :2 * d].astype(_BF16), w[:, 2 * d:].T.astype(_BF16), c, su, sd)


def _diff_attn_kernel(q_ref, k_ref, v_ref, lq1_ref, lk1_ref, lq2_ref, lk2_ref, g_ref, o_ref,
                      s_scr, p_scr, acc_scr, *, lambda_init):
    tqs = s_scr.shape[-1]
    qs = []
    for r in range(q_ref.shape[0] // tqs):
        q = q_ref[r * tqs:(r + 1) * tqs, :]
        lane = lax.broadcasted_iota(_I32, q.shape, 1)
        zero = jnp.zeros_like(q)
        qs += [jnp.where(lane < _DIFF_HEAD_DIM, q, zero), jnp.where(lane >= _DIFF_HEAD_DIM, q, zero)]
    outs = _flash_multi(qs, [k_ref] * len(qs), v_ref, s_scr, p_scr, acc_scr)
    lam = (jnp.exp(jnp.sum(lq1_ref[...] * lk1_ref[...], axis=-1, keepdims=True))
           - jnp.exp(jnp.sum(lq2_ref[...] * lk2_ref[...], axis=-1, keepdims=True)) + lambda_init)
    for r in range(len(qs) // 2):
        o = outs[2 * r] - lam * outs[2 * r + 1]
        o = _rmsnorm(o, g_ref[...]) * (1.0 - lambda_init)
        o_ref[r * tqs:(r + 1) * tqs, :] = o.astype(_BF16)


def _vt_block(vt):
    return pl.BlockSpec((None, None) + vt.shape[2:], lambda b, h, i: (b, h, 0, 0, 0))


def _diff_attn(q, k, vt, lq1, lk1, lq2, lk2, subln_g, batch, seq, lambda_init):
    t, d = q.shape
    tq, tqs = _query_tiles(seq)
    tk = vt.shape[-1]
    nq = seq // tq
    qspec = pl.BlockSpec((tq, _LANES), lambda b, h, i: (b * nq + i, h))
    kspec = pl.BlockSpec((seq, _LANES), lambda b, h, i: (b, h))
    small = lambda n: pl.BlockSpec((1, n), lambda b, h, i: (0, 0))
    hd = _DIFF_HEAD_DIM
    return pl.pallas_call(
        functools.partial(_diff_attn_kernel, lambda_init=lambda_init),
        grid=(batch, _DIFF_HEADS, nq),
        in_specs=[qspec, kspec, _vt_block(vt), small(hd), small(hd), small(hd), small(hd),
                  small(2 * hd)],
        out_specs=qspec,
        out_shape=jax.ShapeDtypeStruct((t, d), _BF16),
        scratch_shapes=_flash_scratch(2 * (tq // tqs), tqs, tk, _LANES),
        compiler_params=_cparams("parallel", "parallel", "arbitrary"),
        name="diff_attn",
    )(q, k, vt, lq1.reshape(1, hd), lk1.reshape(1, hd), lq2.reshape(1, hd), lk2.reshape(1, hd),
      subln_g.reshape(1, 2 * hd))


def _mla_proj_kernel(x_ref, g_ref, wa_ref, gq_ref, wq_ref, gkv_ref, wk_ref, wvt_ref, c_ref, su_ref,
                     sd_ref, q_ref, k_ref, vt_ref, *, scale):
    h = _rmsnorm(x_ref[...], g_ref[...]).astype(_BF16)
    a = jnp.dot(h, wa_ref[...], preferred_element_type=_F32)
    c, su, sd = c_ref[...], su_ref[...], sd_ref[...]
    half = _MLA_ROPE // 2
    kv_lo = _MLA_Q_RANK + _MLA_KV_RANK
    cq = _rmsnorm(a[:, :_MLA_Q_RANK], gq_ref[...]).astype(_BF16)
    ckv = _rmsnorm(a[:, _MLA_Q_RANK:kv_lo], gkv_ref[...]).astype(_BF16)
    k_rope = _rope_lanes(a[:, kv_lo:kv_lo + _LANES], c, su, sd, half)
    qf = jnp.dot(cq, wq_ref[...], preferred_element_type=_F32)
    kf = jnp.dot(ckv, wk_ref[...], preferred_element_type=_F32)
    for j in range(_MLA_HEADS):
        lo, hi = j * _LANES, (j + 1) * _LANES
        q_ref[:, lo:hi] = (_rope_lanes(qf[:, lo:hi], c, su, sd, half) * scale).astype(_BF16)
        k_ref[:, lo:hi] = (kf[:, lo:hi] + k_rope).astype(_BF16)
    _store_vt(vt_ref, lax.dot_general(wvt_ref[...], ckv, _NT, preferred_element_type=_F32),
              _MLA_HEADS * _MLA_V // _LANES)


def _mla_weights(w_a, w_q_b, w_kv_b):
    d = w_a.shape[0]
    kv_lo = _MLA_Q_RANK + _MLA_KV_RANK
    z = lambda *s: jnp.zeros(s, _F32)
    pad = _LANES - _MLA_NOPE - _MLA_ROPE
    wa = jnp.concatenate([w_a[:, :kv_lo], z(d, _MLA_NOPE), w_a[:, kv_lo:], z(d, pad)], axis=1)
    wq = w_q_b.reshape(_MLA_Q_RANK, _MLA_HEADS, _MLA_NOPE + _MLA_ROPE)
    wq = jnp.concatenate([wq, z(_MLA_Q_RANK, _MLA_HEADS, pad)], axis=2)
    wkv = w_kv_b.reshape(_MLA_KV_RANK, _MLA_HEADS, _MLA_NOPE + _MLA_V)
    wk = jnp.concatenate([wkv[:, :, :_MLA_NOPE], z(_MLA_KV_RANK, _MLA_HEADS, _LANES - _MLA_NOPE)], axis=2)
    wvt = wkv[:, :, _MLA_NOPE:].reshape(_MLA_KV_RANK, -1).T
    return (wa.astype(_BF16), wq.reshape(_MLA_Q_RANK, -1).astype(_BF16),
            wk.reshape(_MLA_KV_RANK, -1).astype(_BF16), wvt.astype(_BF16))


def _mla_proj(x, g, w_a, gq, w_q_b, gkv, w_kv_b, batch, seq):
    t, d = x.shape
    tm = _key_chunk(seq)
    nseq = seq // tm
    c, su, sd = _rope_tables(seq, _LANES, _MLA_NOPE, _MLA_ROPE)
    wa, wq, wk, wvt = _mla_weights(w_a, w_q_b, w_kv_b)
    hq = _MLA_HEADS * _LANES
    tab = pl.BlockSpec((tm, _LANES), lambda i: (i % nseq, 0))
    full = lambda a: pl.BlockSpec(a.shape, lambda i: (0, 0))
    row = lambda n: pl.BlockSpec((tm, n), lambda i: (i, 0))
    g, gq, gkv = g.reshape(1, -1), gq.reshape(1, -1), gkv.reshape(1, -1)
    scale = (_MLA_NOPE + _MLA_ROPE) ** -0.5 * _LOG2E
    vt_shape, vt_spec = _vt_shape_spec(batch, _MLA_HEADS * _MLA_V // _LANES, seq, tm)
    return pl.pallas_call(
        functools.partial(_mla_proj_kernel, scale=scale),
        grid=(t // tm,),
        in_specs=[row(d), full(g), full(wa), full(gq), full(wq), full(gkv), full(wk), full(wvt),
                  tab, tab, tab],
        out_specs=[row(hq), row(hq), vt_spec],
        out_shape=[jax.ShapeDtypeStruct((t, hq), _BF16), jax.ShapeDtypeStruct((t, hq), _BF16), vt_shape],
        compiler_params=_cparams("parallel"),
        name="mla_proj",
    )(x, g, wa, gq, wq, gkv, wk, wvt, c, su, sd)


def _mla_attn_kernel(q_ref, ka_ref, kb_ref, v_ref, o_ref, s_scr, p_scr, acc_scr):
    tqs = s_scr.shape[-1]
    n_rows = q_ref.shape[0] // tqs
    qs = []
    for r in range(n_rows):
        qs += [q_ref[r * tqs:(r + 1) * tqs, :_LANES], q_ref[r * tqs:(r + 1) * tqs, _LANES:]]
    outs = _flash_multi(qs, [ka_ref, kb_ref] * n_rows, v_ref, s_scr, p_scr, acc_scr)
    lane = lax.broadcasted_iota(_I32, outs[0].shape, 1)
    for r in range(n_rows):
        o_ref[r * tqs:(r + 1) * tqs, :] = jnp.where(lane < _MLA_V, outs[2 * r],
                                                    outs[2 * r + 1]).astype(_BF16)


def _mla_attn(q, k, vt, batch, seq):
    t = q.shape[0]
    tq, tqs = _query_tiles(seq)
    tk = vt.shape[-1]
    nq = seq // tq
    return pl.pallas_call(
        _mla_attn_kernel,
        grid=(batch, _MLA_HEADS // 2, nq),
        in_specs=[pl.BlockSpec((tq, 2 * _LANES), lambda b, h, i: (b * nq + i, h)),
                  pl.BlockSpec((seq, _LANES), lambda b, h, i: (b, 2 * h)),
                  pl.BlockSpec((seq, _LANES), lambda b, h, i: (b, 2 * h + 1)),
                  _vt_block(vt)],
        out_specs=pl.BlockSpec((tq, _LANES), lambda b, h, i: (b * nq + i, h)),
        out_shape=jax.ShapeDtypeStruct((t, _MLA_HEADS * _MLA_V), _BF16),
        scratch_shapes=_flash_scratch(2 * (tq // tqs), tqs, tk, _LANES),
        compiler_params=_cparams("parallel", "parallel", "arbitrary"),
        name="mla_attn",
    )(q, k, k, vt)


def _oproj_router_kernel(o_ref, w_ref, x_ref, g_ref, rwh_ref, rwl_ref, x1_ref, hn_ref, aff_ref):
    x1 = x_ref[...] + jnp.dot(o_ref[...], w_ref[...], preferred_element_type=_F32)
    x1_ref[...] = x1
    hn = _rmsnorm(x1, g_ref[...])
    _to_row_tiles(hn_ref, hn)
    hh = hn.astype(_BF16)
    hl = (hn - hh.astype(_F32)).astype(_BF16)
    nt = (((1,), (1,)), ((), ()))
    rwh = rwh_ref[...]
    logits = (lax.dot_general(rwh, hh, nt, preferred_element_type=_F32)
              + lax.dot_general(rwh, hl, nt, preferred_element_type=_F32)
              + lax.dot_general(rwl_ref[...], hh, nt, preferred_element_type=_F32))
    e = jnp.exp(logits - jnp.max(logits, axis=0, keepdims=True))
    aff_ref[...] = e / jnp.sum(e, axis=0, keepdims=True)


def _oproj_router(o, w_o, x, g, router_w):
    t, d = x.shape
    tm = _tile(t, 512)
    ne = router_w.shape[1]
    rwt = router_w.T
    rwh = rwt.astype(_BF16)
    rwl = (rwt - rwh.astype(_F32)).astype(_BF16)
    row = pl.BlockSpec((tm, d), lambda i: (i, 0))
    full = lambda a: pl.BlockSpec(a.shape, lambda i: (0, 0))
    g = g.reshape(1, d)
    w = w_o.astype(_BF16)
    return pl.pallas_call(
        _oproj_router_kernel,
        grid=(t // tm,),
        in_specs=[row, full(w), row, full(g), full(rwh), full(rwl)],
        out_specs=[row, pl.BlockSpec((tm * _SUBLANES, _LANES), lambda i: (i, 0)),
                   pl.BlockSpec((ne, tm), lambda i: (0, i))],
        out_shape=[jax.ShapeDtypeStruct((t, d), _F32),
                   jax.ShapeDtypeStruct((t * d // _LANES, _LANES), _F32),
                   jax.ShapeDtypeStruct((ne, t), _F32)],
        compiler_params=_cparams("parallel"),
        name="oproj_router",
    )(o, w, x, g, rwh, rwl)


def _select_kernel(aff_ref, idx_ref, gate_ref, base_ref, thr_scr, *, cap, n_experts, ns):
    rows = aff_ref.shape[0] // n_experts

    def iota(shape, dim):
        return lax.broadcasted_iota(_I32, shape, dim)

    sq = (_LANES, _LANES)
    upto = (iota(sq, 0) <= iota(sq, 1)).astype(_BF16)
    upto_t = (iota(sq, 1) <= iota(sq, 0)).astype(_BF16)
    above = (iota((rows, rows), 1) < iota((rows, rows), 0)).astype(_BF16)
    capf = jnp.float32(cap)
    row_id = iota((rows, ns), 0).astype(_F32)
    lane_id = iota((_LANES, ns), 0).astype(_F32)
    slot_id = iota((1, ns), 1).astype(_F32)

    def lanes(col):
        return jnp.broadcast_to(col, (rows, _LANES))

    def rows_before(tot):
        return jnp.dot(above, lanes(tot).astype(_BF16), preferred_element_type=_F32)[:, :1]

    def step(i, thrs):
        bit = lax.shift_left(jnp.int32(1), jnp.int32(30) - lax.convert_element_type(i, _I32))
        out = []
        for e in range(n_experts):
            bits = pltpu.bitcast(aff_ref[e * rows:(e + 1) * rows, :], _I32)
            cand = thrs[e] | bit
            count = jnp.sum(jnp.where(bits >= cand, 1.0, 0.0), keepdims=True)
            out.append(jnp.where(count >= capf, cand, thrs[e]))
        return tuple(out)

    thrs = lax.fori_loop(0, 31, step, tuple(jnp.zeros((1, 1), _I32) for _ in range(n_experts)))
    for e in range(n_experts):
        thr_scr[e:e + 1, :] = jnp.broadcast_to(thrs[e], (1, _LANES))

    def expert(e, carry):
        r0 = pl.multiple_of(e * rows, rows)
        aff = aff_ref[pl.ds(r0, rows), :]
        bits = pltpu.bitcast(aff, _I32)
        thr = thr_scr[pl.ds(e, 1), :][:, :1]
        gt = bits > thr
        eq = jnp.where(bits == thr, 1.0, 0.0)
        need = capf - jnp.sum(jnp.where(gt, 1.0, 0.0), keepdims=True)
        rank = (rows_before(jnp.sum(eq, axis=-1, keepdims=True))
                + jnp.dot(eq.astype(_BF16), upto, preferred_element_type=_F32) - eq)
        sel = jnp.where(gt | ((eq > 0.0) & (rank < need)), 1.0, 0.0)

        tot = jnp.sum(sel, axis=-1, keepdims=True)
        base = rows_before(tot)
        cum = base + tot
        base_ref[pl.ds(r0, rows), :] = lanes(base).astype(_I32)
        incl_t = jnp.dot(upto_t, sel.T.astype(_BF16), preferred_element_type=_F32).astype(_BF16)
        aff_t = aff.T
        a0 = aff_t.astype(_BF16)
        a1 = (aff_t - a0.astype(_F32)).astype(_BF16)
        a2 = (aff_t - a0.astype(_F32) - a1.astype(_F32)).astype(_BF16)
        for c in range(cap // ns):
            j = slot_id + float(c * ns)
            rj = jnp.sum(jnp.where(cum <= j, 1.0, 0.0), axis=0, keepdims=True)
            hit = row_id == rj
            q = j - jnp.sum(jnp.where(hit, base, 0.0), axis=0, keepdims=True)
            onehot = jnp.where(hit, 1.0, 0.0).astype(_BF16)
            counts = jnp.dot(incl_t, onehot, preferred_element_type=_F32)
            lpos = jnp.sum(jnp.where(counts <= q, 1.0, 0.0), axis=0, keepdims=True)
            arow = (jnp.dot(a0, onehot, preferred_element_type=_F32)
                    + jnp.dot(a1, onehot, preferred_element_type=_F32)
                    + jnp.dot(a2, onehot, preferred_element_type=_F32))
            gate = jnp.sum(jnp.where(lane_id == lpos, arow, 0.0), axis=0, keepdims=True)
            idx_ref[e, pl.ds(c, 1), :] = (rj * float(_LANES) + lpos).astype(_I32)
            gate_ref[e, pl.ds(c, 1), :] = gate
        return carry

    lax.fori_loop(0, n_experts, expert, 0)


def _select(aff_t, cap):
    ne, t = aff_t.shape
    rows = t // _LANES
    ns = _tile(cap, 512)
    aff2 = aff_t.reshape(ne * rows, _LANES)
    listing = jax.ShapeDtypeStruct((ne, cap // ns, ns), _I32)
    idx, gates, base = pl.pallas_call(
        functools.partial(_select_kernel, cap=cap, n_experts=ne, ns=ns),
        out_shape=[listing, jax.ShapeDtypeStruct(listing.shape, _F32),
                   jax.ShapeDtypeStruct(aff2.shape, _I32)],
        scratch_shapes=[pltpu.VMEM((ne, _LANES), _I32)],
        compiler_params=pltpu.CompilerParams(vmem_limit_bytes=_VMEM_LIMIT),
        name="expert_select",
    )(aff2)
    return idx.reshape(ne, cap), gates.reshape(ne, cap), base.reshape(ne, rows, _LANES)[:, :, 0]


_SUBLANES = 8


def _to_row_tiles(dst_ref, x):
    n = x.shape[0]
    for j in range(x.shape[1] // _LANES):
        dst_ref[pl.ds(j, n, stride=_SUBLANES), :] = x[:, j * _LANES:(j + 1) * _LANES]


def _from_row_tiles(src_ref, n):
    return jnp.concatenate([src_ref[pl.ds(j, n, stride=_SUBLANES), :] for j in range(_SUBLANES)],
                           axis=1)


def _ffn_kernel(idx_ref, nxt_ref, hn_hbm, gate_ref, wg_ref, wu_ref, wd_ref, ye_ref, xbuf, sems,
                *, f_chunk):
    tc = xbuf.shape[1] // _SUBLANES
    n_tiles = pl.num_programs(1)
    step = pl.program_id(0) * n_tiles + pl.program_id(1)
    last = pl.num_programs(0) * n_tiles - 1
    slot = step % 2

    def row_copy(j, t, buf):
        src = hn_hbm.at[pl.ds(pl.multiple_of(t * _SUBLANES, _SUBLANES), _SUBLANES)]
        dst = xbuf.at[buf, pl.ds(pl.multiple_of(j * _SUBLANES, _SUBLANES), _SUBLANES)]
        return pltpu.make_async_copy(src, dst, sems.at[buf])

    def gather(ids_ref, buf, unroll):
        def issue(j, carry):
            row_copy(j, ids_ref[0, 0, j], buf).start()
            return carry
        lax.fori_loop(0, tc, issue, 0, unroll=unroll)

    @pl.when(step == 0)
    def _():
        gather(idx_ref, 0, 8)

    @pl.when(step < last)
    def _():
        gather(nxt_ref, 1 - slot, True)

    def drain(j, carry):
        row_copy(j, 0, slot).wait()
        return carry

    lax.fori_loop(0, tc, drain, 0, unroll=True)
    x = _from_row_tiles(xbuf.at[slot], tc).astype(_BF16)
    d_ff = wg_ref.shape[-1]
    y = jnp.zeros((tc, wd_ref.shape[-1]), _F32)
    for f in range(0, d_ff, f_chunk):
        g = jnp.dot(x, wg_ref[0, :, f:f + f_chunk], preferred_element_type=_F32)
        u = jnp.dot(x, wu_ref[0, :, f:f + f_chunk], preferred_element_type=_F32)
        hid = (g * jax.nn.sigmoid(g) * u).astype(_BF16)
        y = y + jnp.dot(hid, wd_ref[0, f:f + f_chunk, :], preferred_element_type=_F32)
    _to_row_tiles(ye_ref, y * gate_ref[0])


def _expert_ffn(hn, idx, gates, wg, wu, wd, layer):
    ne, cap = idx.shape
    d, d_ff = wg.shape[-2:]
    tc = _tile(cap, 512)
    nc = cap // tc
    last = ne * nc - 1
    ids = idx.reshape(ne * nc, 1, tc)
    return pl.pallas_call(
        functools.partial(_ffn_kernel, f_chunk=_tile(d_ff, 512)),
        grid=(ne, nc),
        in_specs=[pl.BlockSpec((1, 1, tc), lambda e, j: (e * nc + j, 0, 0), memory_space=pltpu.SMEM),
                  pl.BlockSpec((1, 1, tc), lambda e, j: (jnp.minimum(e * nc + j + 1, last), 0, 0),
                               memory_space=pltpu.SMEM),
                  pl.BlockSpec(memory_space=pl.ANY),
                  pl.BlockSpec((1, tc, 1), lambda e, j: (e, j, 0)),
                  pl.BlockSpec((None, 1, d, d_ff), lambda e, j: (layer, e, 0, 0)),
                  pl.BlockSpec((None, 1, d, d_ff), lambda e, j: (layer, e, 0, 0)),
                  pl.BlockSpec((None, 1, d_ff, d), lambda e, j: (layer, e, 0, 0))],
        out_specs=pl.BlockSpec((tc * _SUBLANES, _LANES), lambda e, j: (e * nc + j, 0)),
        out_shape=jax.ShapeDtypeStruct((ne * cap * d // _LANES, _LANES), _F32),
        scratch_shapes=[pltpu.VMEM((2, tc * _SUBLANES, _LANES), _F32), pltpu.SemaphoreType.DMA((2,))],
        compiler_params=_cparams("arbitrary", "arbitrary"),
        name="expert_ffn",
    )(ids, ids, hn, gates.reshape(ne, cap, 1), wg, wu, wd)


_ROW_CHUNK = 8
_RMW_GROUP = 8


def _div(x, n):
    if n & (n - 1) == 0:
        return lax.shift_right_logical(x, jnp.int32(n.bit_length() - 1))
    return lax.div(x, jnp.int32(n))


def _combine_kernel(lo_ref, x_ref, ye_hbm, *rest, n_experts, cap, idx_block, final_norm):
    idx_refs, rest = rest[:n_experts], rest[n_experts:]
    if final_norm:
        g_ref, rest = rest[0], rest[1:]
    out_ref, acc, stage, sems = rest
    tb = x_ref.shape[0]
    region = stage.shape[1] // (n_experts * _SUBLANES)
    b = pl.program_id(0)
    slot = b % 2

    def tile_of(row):
        return pl.ds(pl.multiple_of(row * _SUBLANES, _SUBLANES), _SUBLANES)

    def for_each_chunk(blk, buf, fn):
        size = _ROW_CHUNK * _SUBLANES
        for e in range(n_experts):
            lo, hi = lo_ref[e, blk], lo_ref[e, blk + 1]
            c0 = _div(lo, _ROW_CHUNK)

            def one(c, carry):
                src = pl.multiple_of((e * cap + c * _ROW_CHUNK) * _SUBLANES, size)
                dst = pl.multiple_of((e * region + (c - c0) * _ROW_CHUNK) * _SUBLANES, size)
                fn(pltpu.make_async_copy(ye_hbm.at[pl.ds(src, size)],
                                         stage.at[buf, pl.ds(dst, size)], sems.at[buf]))
                return carry

            lax.fori_loop(c0, _div(hi + (_ROW_CHUNK - 1), _ROW_CHUNK), one, 0)

    @pl.when(b == 0)
    def _():
        for_each_chunk(0, 0, lambda cp: cp.start())

    @pl.when(b + 1 < pl.num_programs(0))
    def _():
        for_each_chunk(b + 1, 1 - slot, lambda cp: cp.start())

    _to_row_tiles(acc, x_ref[...])
    for_each_chunk(b, slot, lambda cp: cp.wait())
    for e in range(n_experts):
        lo, hi = lo_ref[e, b], lo_ref[e, b + 1]
        first = _div(lo, idx_block) * idx_block
        row0 = e * region - _div(lo, _ROW_CHUNK) * _ROW_CHUNK
        ids = idx_refs[e]

        def token(j):
            return ids[0, 0, 0, j - first] - b * tb

        def add_rows(j0, count):
            toks = [tile_of(token(j0 + i)) for i in range(count)]
            sums = [acc[toks[i], :] + stage[slot, tile_of(row0 + j0 + i), :] for i in range(count)]
            for i in range(count):
                acc[toks[i], :] = sums[i]

        def group(g, carry):
            add_rows(lo + g * _RMW_GROUP, _RMW_GROUP)
            return carry

        def single(j, carry):
            add_rows(j, 1)
            return carry

        n_groups = _div(hi - lo, _RMW_GROUP)
        lax.fori_loop(0, n_groups, group, 0)
        lax.fori_loop(lo + n_groups * _RMW_GROUP, hi, single, 0)

    y = _from_row_tiles(acc, tb)
    out_ref[...] = _rmsnorm(y, g_ref[...]) if final_norm else y


def _combine(x1, ye, idx, before, final_g=None):
    t, d = x1.shape
    ne, cap = idx.shape
    assert d == _SUBLANES * _LANES
    tb = _tile(t, 2 * _LANES)
    nb = t // tb
    assert cap % _ROW_CHUNK == 0
    idx_block = min(tb, cap)
    n_ib = cap // idx_block
    lo = jnp.concatenate([before[:, ::tb // _LANES], jnp.full((ne, 1), cap, _I32)], axis=1)
    idx3 = idx.reshape(ne, n_ib, idx_block)
    windows = jnp.concatenate([idx3, jnp.concatenate([idx3[:, 1:], idx3[:, -1:]], axis=1)], axis=2)
    windows = windows.reshape(ne, n_ib, 1, 2 * idx_block)

    def idx_spec(e):
        return pl.BlockSpec((1, 1, 1, 2 * idx_block),
                            lambda b, lo_ref: (e, _div(lo_ref[e, b], idx_block), 0, 0),
                            memory_space=pltpu.SMEM)

    region = tb + _ROW_CHUNK
    row = pl.BlockSpec((tb, d), lambda b, lo_ref: (b, 0))
    norm_specs, norm_args = [], []
    if final_g is not None:
        norm_specs, norm_args = [pl.BlockSpec((1, d), lambda b, lo_ref: (0, 0))], [final_g.reshape(1, d)]
    grid_spec = pltpu.PrefetchScalarGridSpec(
        num_scalar_prefetch=1,
        grid=(nb,),
        in_specs=[row, pl.BlockSpec(memory_space=pl.ANY)] + [idx_spec(e) for e in range(ne)]
                 + norm_specs,
        out_specs=row,
        scratch_shapes=[pltpu.VMEM((tb * _SUBLANES, _LANES), _F32),
                        pltpu.VMEM((2, ne * region * _SUBLANES, _LANES), _F32),
                        pltpu.SemaphoreType.DMA((2,))])
    return pl.pallas_call(
        functools.partial(_combine_kernel, n_experts=ne, cap=cap, idx_block=idx_block,
                          final_norm=final_g is not None),
        grid_spec=grid_spec,
        out_shape=jax.ShapeDtypeStruct((t, d), _F32),
        compiler_params=_cparams("arbitrary"),
        name="moe_combine",
    )(lo, x1, ye, *([windows] * ne), *norm_args)


def _moe(x1, hn, aff_t, wg, wu, wd, layer, final_g):
    t, d = x1.shape
    ne = aff_t.shape[0]
    cap = max(1, _CAPACITY_FACTOR * t // ne)
    idx, gates, before = _select(aff_t, cap)
    ye = _expert_ffn(hn, idx, gates, wg, wu, wd, layer)
    return _combine(x1, ye, idx, before, final_g)


def _trunk(x, p):
    batch, seq, d = x.shape
    x = x.reshape(batch * seq, d)
    depth = p["ffn_norm_g"].shape[0]
    for i in range(depth):
        j = i // _N_MIXERS
        if i % _N_MIXERS == 0:
            lambda_init = 0.8 - 0.6 * float(np.exp(-0.3 * i))
            q, k, v = _diff_qkv(x, p["diff_norm_g"][j], p["diff_w_qkv"][j], batch, seq)
            o = _diff_attn(q, k, v, p["diff_lambda_q1"][j], p["diff_lambda_k1"][j],
                           p["diff_lambda_q2"][j], p["diff_lambda_k2"][j], p["diff_subln_g"][j],
                           batch, seq, lambda_init)
            w_o = p["diff_w_o"][j]
        else:
            q, k, v = _mla_proj(x, p["mla_norm_g"][j], p["mla_w_a"][j], p["mla_q_norm_g"][j],
                                p["mla_w_q_b"][j], p["mla_kv_norm_g"][j], p["mla_w_kv_b"][j],
                                batch, seq)
            o = _mla_attn(q, k, v, batch, seq)
            w_o = p["mla_w_o"][j]
        x1, hn, aff_t = _oproj_router(o, w_o, x, p["ffn_norm_g"][i], p["router_w"][i])
        x = _moe(x1, hn, aff_t, p["w_gate_bf16"], p["w_up_bf16"], p["w_down_bf16"], i,
                 p["final_norm_g"] if i == depth - 1 else None)
    return x.reshape(batch, seq, d)


def kernel(x_prompt, x_sample, diff_norm_g, diff_w_qkv, diff_lambda_q1, diff_lambda_k1, diff_lambda_q2,
           diff_lambda_k2, diff_subln_g, diff_w_o, mla_norm_g, mla_w_a, mla_q_norm_g, mla_w_q_b,
           mla_kv_norm_g, mla_w_kv_b, mla_w_o, ffn_norm_g, router_w, w_gate, w_up, w_down, final_norm_g):
    p = dict(diff_norm_g=diff_norm_g, diff_w_qkv=diff_w_qkv, diff_lambda_q1=diff_lambda_q1,
             diff_lambda_k1=diff_lambda_k1, diff_lambda_q2=diff_lambda_q2, diff_lambda_k2=diff_lambda_k2,
             diff_subln_g=diff_subln_g, diff_w_o=diff_w_o, mla_norm_g=mla_norm_g, mla_w_a=mla_w_a,
             mla_q_norm_g=mla_q_norm_g, mla_w_q_b=mla_w_q_b, mla_kv_norm_g=mla_kv_norm_g,
             mla_w_kv_b=mla_w_kv_b, mla_w_o=mla_w_o, ffn_norm_g=ffn_norm_g, router_w=router_w,
             final_norm_g=final_norm_g, w_gate_bf16=w_gate.astype(_BF16), w_up_bf16=w_up.astype(_BF16),
             w_down_bf16=w_down.astype(_BF16))
    return _trunk(x_prompt, p), _trunk(x_sample, p)
```

```python
import functools

import numpy as np
import jax
import jax.numpy as jnp
from jax import lax
from jax.experimental import pallas as pl
from jax.experimental.pallas import tpu as pltpu

_F32, _BF16, _I32 = jnp.float32, jnp.bfloat16, jnp.int32
_EPS = 1e-6
_ROPE_THETA = 500000.0
_LANES = 128
_NEG = -1e30
_VMEM_LIMIT = 56 * 1024 * 1024

_DIFF_HEADS, _DIFF_HEAD_DIM, _DIFF_ROT = 8, 64, 16
_MLA_HEADS, _MLA_NOPE, _MLA_ROPE, _MLA_V = 16, 64, 32, 64
_MLA_Q_RANK, _MLA_KV_RANK = 384, 256
_N_EXPERTS, _CAPACITY_FACTOR = 16, 2
_N_MIXERS = 2


_TOKEN_TILE = 512
_QUERY_STREAM = 256
_STREAMS_PER_HEAD = 2
_SLOT_TILE = 512
_FF_CHUNK = 512
_COMBINE_BLOCK = 256


def _cparams(*sem):
    return pltpu.CompilerParams(dimension_semantics=sem, vmem_limit_bytes=_VMEM_LIMIT)


def _tile(n, pref):
    t = min(n, pref)
    assert n % t == 0, (n, pref)
    return t


def _rmsnorm(x, g):
    return x * lax.rsqrt(jnp.mean(x * x, axis=-1, keepdims=True) + _EPS) * g


def _rope_lanes(y, c, s_up, s_dn, half):
    return y * c + pltpu.roll(y, _LANES - half, 1) * s_up + pltpu.roll(y, half, 1) * s_dn


def _rope_tables(seq, group, start, rot):
    half = rot // 2
    pos = jnp.arange(seq, dtype=_F32)
    inv = jnp.float32(_ROPE_THETA) ** (-jnp.arange(0, rot, 2, dtype=_F32) / rot)
    ang = pos[:, None] * inv[None, :]
    cos, sin = jnp.cos(ang), jnp.sin(ang)
    j = (np.arange(_LANES) % group) - start
    first = (j >= 0) & (j < half)
    second = (j >= half) & (j < rot)
    f = np.where(first, j, np.where(second, j - half, 0))
    c = jnp.where(first | second, cos[:, f], 1.0)
    s_up = jnp.where(first, -sin[:, f], 0.0)
    s_dn = jnp.where(second, sin[:, f], 0.0)
    return c, s_up, s_dn


_LOG2E = 1.4426950408889634
_ONES_ROWS = 16
_CHUNK_UNROLL = 2


def _key_chunk(seq):
    return _tile(seq, _TOKEN_TILE)


def _query_tiles(seq):
    tqs = _tile(seq, _QUERY_STREAM)
    return _tile(seq, _STREAMS_PER_HEAD * tqs), tqs


def _flash_scratch(n, tq, tk, dv):
    return [pltpu.VMEM((n, tk, tq), _F32), pltpu.VMEM((n, tk, tq), _BF16),
            pltpu.VMEM((n, dv + _ONES_ROWS, tq), _F32)]


def _flash_multi(qs, k_refs, vt_ref, s_scr, p_scr, acc_scr):
    n, tk, tq = s_scr.shape
    n_chunks = vt_ref.shape[0]
    dv = vt_ref.shape[1] - _ONES_ROWS

    def scores(j):
        out = []
        for c in range(n):
            ks = k_refs[c][pl.ds(pl.multiple_of(j * tk, tk), tk), :]
            s = lax.dot_general(ks, qs[c], (((1,), (1,)), ((), ())), preferred_element_type=_F32)
            s_scr[c] = s
            out.append(jnp.max(s, axis=0, keepdims=True))
        return out

    def softmax(smax, st):
        out = []
        for c in range(n):
            m_new = jnp.maximum(st[c][0], smax[c])
            alpha = jnp.exp2(st[c][0] - m_new)
            p_scr[c] = jnp.exp2(s_scr[c] - m_new).astype(_BF16)
            out.append((m_new, alpha))
        return out

    def accumulate(st, j):
        for c in range(n):
            acc_scr[c] = st[c][1] * acc_scr[c] + jnp.dot(vt_ref[j], p_scr[c],
                                                         preferred_element_type=_F32)

    acc_scr[...] = jnp.zeros(acc_scr.shape, _F32)
    st = [(jnp.full((1, tq), _NEG, _F32), None)] * n
    st = softmax(scores(0), st)
    if n_chunks > 1:
        def body(j, carry):
            smax, st = carry
            accumulate(st, j - 1)
            st = softmax(smax, st)
            return scores(j + 1), st

        unroll = _CHUNK_UNROLL if n_chunks - 2 >= 2 * _CHUNK_UNROLL else 1
        smax, st = lax.fori_loop(1, n_chunks - 1, body, (scores(1), st), unroll=unroll)
        accumulate(st, n_chunks - 2)
        st = softmax(smax, st)
    accumulate(st, n_chunks - 1)
    return [(acc_scr[c, :dv, :] * (1.0 / acc_scr[c, dv:dv + 1, :])).T for c in range(n)]


_NT = (((1,), (1,)), ((), ()))


def _store_vt(vt_ref, vt, heads):
    ones = jnp.ones((_ONES_ROWS, vt.shape[1]), _BF16)
    for j in range(heads):
        vt_ref[0, j, 0, :_LANES, :] = vt[j * _LANES:(j + 1) * _LANES, :].astype(_BF16)
        vt_ref[0, j, 0, _LANES:, :] = ones


def _diff_qkv_kernel(x_ref, g_ref, w_ref, wvt_ref, c_ref, su_ref, sd_ref, q_ref, k_ref, vt_ref,
                     *, scale):
    h = _rmsnorm(x_ref[...], g_ref[...]).astype(_BF16)
    y = jnp.dot(h, w_ref[...], preferred_element_type=_F32)
    d = q_ref.shape[-1]
    c, su, sd = c_ref[...], su_ref[...], sd_ref[...]
    half = _DIFF_ROT // 2
    for j in range(d // _LANES):
        lo, hi = j * _LANES, (j + 1) * _LANES
        q_ref[:, lo:hi] = (_rope_lanes(y[:, lo:hi], c, su, sd, half) * scale).astype(_BF16)
        k_ref[:, lo:hi] = _rope_lanes(y[:, d + lo:d + hi], c, su, sd, half).astype(_BF16)
    _store_vt(vt_ref, lax.dot_general(wvt_ref[...], h, _NT, preferred_element_type=_F32), _DIFF_HEADS)


def _vt_shape_spec(batch, heads, seq, tm):
    nseq = seq // tm
    rows = _LANES + _ONES_ROWS
    return (jax.ShapeDtypeStruct((batch, heads, nseq, rows, tm), _BF16),
            pl.BlockSpec((1, heads, 1, rows, tm), lambda i: (i // nseq, 0, i % nseq, 0, 0)))


def _diff_qkv(x, g, w, batch, seq):
    t, d = x.shape
    tm = _key_chunk(seq)
    c, su, sd = _rope_tables(seq, _DIFF_HEAD_DIM, 0, _DIFF_ROT)
    nseq = seq // tm
    tab = pl.BlockSpec((tm, _LANES), lambda i: (i % nseq, 0))
    row = pl.BlockSpec((tm, d), lambda i: (i, 0))
    out = jax.ShapeDtypeStruct((t, d), _BF16)
    vt_shape, vt_spec = _vt_shape_spec(batch, _DIFF_HEADS, seq, tm)
    return pl.pallas_call(
        functools.partial(_diff_qkv_kernel, scale=_DIFF_HEAD_DIM ** -0.5 * _LOG2E),
        grid=(t // tm,),
        in_specs=[row, pl.BlockSpec((1, d), lambda i: (0, 0)),
                  pl.BlockSpec((d, 2 * d), lambda i: (0, 0)),
                  pl.BlockSpec((d, d), lambda i: (0, 0)), tab, tab, tab],
        out_specs=[row, row, vt_spec],
        out_shape=[out, out, vt_shape],
        compiler_params=_cparams("parallel"),
        name="diff_qkv",
    )(x, g.reshape(1, d), w[:, :2 * d].astype(_BF16), w[:, 2 * d:].T.astype(_BF16), c, su, sd)


def _diff_attn_kernel(q_ref, k_ref, v_ref, lq1_ref, lk1_ref, lq2_ref, lk2_ref, g_ref, o_ref,
                      s_scr, p_scr, acc_scr, *, lambda_init):
    tqs = s_scr.shape[-1]
    qs = []
    for r in range(q_ref.shape[0] // tqs):
        q = q_ref[r * tqs:(r + 1) * tqs, :]
        lane = lax.broadcasted_iota(_I32, q.shape, 1)
        zero = jnp.zeros_like(q)
        qs += [jnp.where(lane < _DIFF_HEAD_DIM, q, zero), jnp.where(lane >= _DIFF_HEAD_DIM, q, zero)]
    outs = _flash_multi(qs, [k_ref] * len(qs), v_ref, s_scr, p_scr, acc_scr)
    lam = (jnp.exp(jnp.sum(lq1_ref[...] * lk1_ref[...], axis=-1, keepdims=True))
           - jnp.exp(jnp.sum(lq2_ref[...] * lk2_ref[...], axis=-1, keepdims=True)) + lambda_init)
    for r in range(len(qs) // 2):
        o = outs[2 * r] - lam * outs[2 * r + 1]
        o = _rmsnorm(o, g_ref[...]) * (1.0 - lambda_init)
        o_ref[r * tqs:(r + 1) * tqs, :] = o.astype(_BF16)


def _vt_block(vt):
    return pl.BlockSpec((None, None) + vt.shape[2:], lambda b, h, i: (b, h, 0, 0, 0))


def _diff_attn(q, k, vt, lq1, lk1, lq2, lk2, subln_g, batch, seq, lambda_init):
    t, d = q.shape
    tq, tqs = _query_tiles(seq)
    tk = vt.shape[-1]
    nq = seq // tq
    qspec = pl.BlockSpec((tq, _LANES), lambda b, h, i: (b * nq + i, h))
    kspec = pl.BlockSpec((seq, _LANES), lambda b, h, i: (b, h))
    small = lambda n: pl.BlockSpec((1, n), lambda b, h, i: (0, 0))
    hd = _DIFF_HEAD_DIM
    return pl.pallas_call(
        functools.partial(_diff_attn_kernel, lambda_init=lambda_init),
        grid=(batch, _DIFF_HEADS, nq),
        in_specs=[qspec, kspec, _vt_block(vt), small(hd), small(hd), small(hd), small(hd),
                  small(2 * hd)],
        out_specs=qspec,
        out_shape=jax.ShapeDtypeStruct((t, d), _BF16),
        scratch_shapes=_flash_scratch(2 * (tq // tqs), tqs, tk, _LANES),
        compiler_params=_cparams("parallel", "parallel", "arbitrary"),
        name="diff_attn",
    )(q, k, vt, lq1.reshape(1, hd), lk1.reshape(1, hd), lq2.reshape(1, hd), lk2.reshape(1, hd),
      subln_g.reshape(1, 2 * hd))


def _mla_proj_kernel(x_ref, g_ref, wa_ref, gq_ref, wq_ref, gkv_ref, wk_ref, wvt_ref, c_ref, su_ref,
                     sd_ref, q_ref, k_ref, vt_ref, *, scale):
    h = _rmsnorm(x_ref[...], g_ref[...]).astype(_BF16)
    a = jnp.dot(h, wa_ref[...], preferred_element_type=_F32)
    c, su, sd = c_ref[...], su_ref[...], sd_ref[...]
    half = _MLA_ROPE // 2
    kv_lo = _MLA_Q_RANK + _MLA_KV_RANK
    cq = _rmsnorm(a[:, :_MLA_Q_RANK], gq_ref[...]).astype(_BF16)
    ckv = _rmsnorm(a[:, _MLA_Q_RANK:kv_lo], gkv_ref[...]).astype(_BF16)
    k_rope = _rope_lanes(a[:, kv_lo:kv_lo + _LANES], c, su, sd, half)
    qf = jnp.dot(cq, wq_ref[...], preferred_element_type=_F32)
    kf = jnp.dot(ckv, wk_ref[...], preferred_element_type=_F32)
    for j in range(_MLA_HEADS):
        lo, hi = j * _LANES, (j + 1) * _LANES
        q_ref[:, lo:hi] = (_rope_lanes(qf[:, lo:hi], c, su, sd, half) * scale).astype(_BF16)
        k_ref[:, lo:hi] = (kf[:, lo:hi] + k_rope).astype(_BF16)
    _store_vt(vt_ref, lax.dot_general(wvt_ref[...], ckv, _NT, preferred_element_type=_F32),
              _MLA_HEADS * _MLA_V // _LANES)


def _mla_weights(w_a, w_q_b, w_kv_b):
    d = w_a.shape[0]
    kv_lo = _MLA_Q_RANK + _MLA_KV_RANK
    z = lambda *s: jnp.zeros(s, _F32)
    pad = _LANES - _MLA_NOPE - _MLA_ROPE
    wa = jnp.concatenate([w_a[:, :kv_lo], z(d, _MLA_NOPE), w_a[:, kv_lo:], z(d, pad)], axis=1)
    wq = w_q_b.reshape(_MLA_Q_RANK, _MLA_HEADS, _MLA_NOPE + _MLA_ROPE)
    wq = jnp.concatenate([wq, z(_MLA_Q_RANK, _MLA_HEADS, pad)], axis=2)
    wkv = w_kv_b.reshape(_MLA_KV_RANK, _MLA_HEADS, _MLA_NOPE + _MLA_V)
    wk = jnp.concatenate([wkv[:, :, :_MLA_NOPE], z(_MLA_KV_RANK, _MLA_HEADS, _LANES - _MLA_NOPE)], axis=2)
    wvt = wkv[:, :, _MLA_NOPE:].reshape(_MLA_KV_RANK, -1).T
    return (wa.astype(_BF16), wq.reshape(_MLA_Q_RANK, -1).astype(_BF16),
            wk.reshape(_MLA_KV_RANK, -1).astype(_BF16), wvt.astype(_BF16))


def _mla_proj(x, g, w_a, gq, w_q_b, gkv, w_kv_b, batch, seq):
    t, d = x.shape
    tm = _key_chunk(seq)
    nseq = seq // tm
    c, su, sd = _rope_tables(seq, _LANES, _MLA_NOPE, _MLA_ROPE)
    wa, wq, wk, wvt = _mla_weights(w_a, w_q_b, w_kv_b)
    hq = _MLA_HEADS * _LANES
    tab = pl.BlockSpec((tm, _LANES), lambda i: (i % nseq, 0))
    full = lambda a: pl.BlockSpec(a.shape, lambda i: (0, 0))
    row = lambda n: pl.BlockSpec((tm, n), lambda i: (i, 0))
    g, gq, gkv = g.reshape(1, -1), gq.reshape(1, -1), gkv.reshape(1, -1)
    scale = (_MLA_NOPE + _MLA_ROPE) ** -0.5 * _LOG2E
    vt_shape, vt_spec = _vt_shape_spec(batch, _MLA_HEADS * _MLA_V // _LANES, seq, tm)
    return pl.pallas_call(
        functools.partial(_mla_proj_kernel, scale=scale),
        grid=(t // tm,),
        in_specs=[row(d), full(g), full(wa), full(gq), full(wq), full(gkv), full(wk), full(wvt),
                  tab, tab, tab],
        out_specs=[row(hq), row(hq), vt_spec],
        out_shape=[jax.ShapeDtypeStruct((t, hq), _BF16), jax.ShapeDtypeStruct((t, hq), _BF16), vt_shape],
        compiler_params=_cparams("parallel"),
        name="mla_proj",
    )(x, g, wa, gq, wq, gkv, wk, wvt, c, su, sd)


def _mla_attn_kernel(q_ref, ka_ref, kb_ref, v_ref, o_ref, s_scr, p_scr, acc_scr):
    tqs = s_scr.shape[-1]
    n_rows = q_ref.shape[0] // tqs
    qs = []
    for r in range(n_rows):
        qs += [q_ref[r * tqs:(r + 1) * tqs, :_LANES], q_ref[r * tqs:(r + 1) * tqs, _LANES:]]
    outs = _flash_multi(qs, [ka_ref, kb_ref] * n_rows, v_ref, s_scr, p_scr, acc_scr)
    lane = lax.broadcasted_iota(_I32, outs[0].shape, 1)
    for r in range(n_rows):
        o_ref[r * tqs:(r + 1) * tqs, :] = jnp.where(lane < _MLA_V, outs[2 * r],
                                                    outs[2 * r + 1]).astype(_BF16)


def _mla_attn(q, k, vt, batch, seq):
    t = q.shape[0]
    tq, tqs = _query_tiles(seq)
    tk = vt.shape[-1]
    nq = seq // tq
    return pl.pallas_call(
        _mla_attn_kernel,
        grid=(batch, _MLA_HEADS // 2, nq),
        in_specs=[pl.BlockSpec((tq, 2 * _LANES), lambda b, h, i: (b * nq + i, h)),
                  pl.BlockSpec((seq, _LANES), lambda b, h, i: (b, 2 * h)),
                  pl.BlockSpec((seq, _LANES), lambda b, h, i: (b, 2 * h + 1)),
                  _vt_block(vt)],
        out_specs=pl.BlockSpec((tq, _LANES), lambda b, h, i: (b * nq + i, h)),
        out_shape=jax.ShapeDtypeStruct((t, _MLA_HEADS * _MLA_V), _BF16),
        scratch_shapes=_flash_scratch(2 * (tq // tqs), tqs, tk, _LANES),
        compiler_params=_cparams("parallel", "parallel", "arbitrary"),
        name="mla_attn",
    )(q, k, k, vt)


def _oproj_router_kernel(o_ref, w_ref, x_ref, g_ref, rwh_ref, rwl_ref, x1_ref, hn_ref, aff_ref):
    x1 = x_ref[...] + jnp.dot(o_ref[...], w_ref[...], preferred_element_type=_F32)
    x1_ref[...] = x1
    hn = _rmsnorm(x1, g_ref[...])
    _to_row_tiles(hn_ref, hn)
    hh = hn.astype(_BF16)
    hl = (hn - hh.astype(_F32)).astype(_BF16)
    nt = (((1,), (1,)), ((), ()))
    rwh = rwh_ref[...]
    logits = (lax.dot_general(rwh, hh, nt, preferred_element_type=_F32)
              + lax.dot_general(rwh, hl, nt, preferred_element_type=_F32)
              + lax.dot_general(rwl_ref[...], hh, nt, preferred_element_type=_F32))
    e = jnp.exp(logits - jnp.max(logits, axis=0, keepdims=True))
    aff_ref[...] = e / jnp.sum(e, axis=0, keepdims=True)


def _oproj_router(o, w_o, x, g, router_w):
    t, d = x.shape
    tm = _tile(t, _TOKEN_TILE)
    ne = router_w.shape[1]
    rwt = router_w.T
    rwh = rwt.astype(_BF16)
    rwl = (rwt - rwh.astype(_F32)).astype(_BF16)
    row = pl.BlockSpec((tm, d), lambda i: (i, 0))
    full = lambda a: pl.BlockSpec(a.shape, lambda i: (0, 0))
    g = g.reshape(1, d)
    w = w_o.astype(_BF16)
    return pl.pallas_call(
        _oproj_router_kernel,
        grid=(t // tm,),
        in_specs=[row, full(w), row, full(g), full(rwh), full(rwl)],
        out_specs=[row, pl.BlockSpec((tm * _SUBLANES, _LANES), lambda i: (i, 0)),
                   pl.BlockSpec((ne, tm), lambda i: (0, i))],
        out_shape=[jax.ShapeDtypeStruct((t, d), _F32),
                   jax.ShapeDtypeStruct((t * d // _LANES, _LANES), _F32),
                   jax.ShapeDtypeStruct((ne, t), _F32)],
        compiler_params=_cparams("parallel"),
        name="oproj_router",
    )(o, w, x, g, rwh, rwl)


_MAGNITUDE_BITS = 31


def _select_kernel(aff_ref, idx_ref, gate_ref, base_ref, thr_scr, *, cap, n_experts, ns):
    rows = aff_ref.shape[0] // n_experts

    def iota(shape, dim):
        return lax.broadcasted_iota(_I32, shape, dim)

    sq = (_LANES, _LANES)
    upto = (iota(sq, 0) <= iota(sq, 1)).astype(_BF16)
    upto_t = (iota(sq, 1) <= iota(sq, 0)).astype(_BF16)
    above = (iota((rows, rows), 1) < iota((rows, rows), 0)).astype(_BF16)
    capf = jnp.float32(cap)
    row_id = iota((rows, ns), 0).astype(_F32)
    lane_id = iota((_LANES, ns), 0).astype(_F32)
    slot_id = iota((1, ns), 1).astype(_F32)

    def lanes(col):
        return jnp.broadcast_to(col, (rows, _LANES))

    def rows_before(tot):
        return jnp.dot(above, lanes(tot).astype(_BF16), preferred_element_type=_F32)[:, :1]

    def step(i, thrs):
        bit = lax.shift_left(jnp.int32(1),
                             jnp.int32(_MAGNITUDE_BITS - 1) - lax.convert_element_type(i, _I32))
        out = []
        for e in range(n_experts):
            bits = pltpu.bitcast(aff_ref[e * rows:(e + 1) * rows, :], _I32)
            cand = thrs[e] | bit
            count = jnp.sum(jnp.where(bits >= cand, 1.0, 0.0), keepdims=True)
            out.append(jnp.where(count >= capf, cand, thrs[e]))
        return tuple(out)

    thrs = lax.fori_loop(0, _MAGNITUDE_BITS, step,
                         tuple(jnp.zeros((1, 1), _I32) for _ in range(n_experts)))
    for e in range(n_experts):
        thr_scr[e:e + 1, :] = jnp.broadcast_to(thrs[e], (1, _LANES))

    def expert(e, carry):
        r0 = pl.multiple_of(e * rows, rows)
        aff = aff_ref[pl.ds(r0, rows), :]
        bits = pltpu.bitcast(aff, _I32)
        thr = thr_scr[pl.ds(e, 1), :][:, :1]
        gt = bits > thr
        eq = jnp.where(bits == thr, 1.0, 0.0)
        need = capf - jnp.sum(jnp.where(gt, 1.0, 0.0), keepdims=True)
        rank = (rows_before(jnp.sum(eq, axis=-1, keepdims=True))
                + jnp.dot(eq.astype(_BF16), upto, preferred_element_type=_F32) - eq)
        sel = jnp.where(gt | ((eq > 0.0) & (rank < need)), 1.0, 0.0)

        tot = jnp.sum(sel, axis=-1, keepdims=True)
        base = rows_before(tot)
        cum = base + tot
        base_ref[pl.ds(r0, rows), :] = lanes(base).astype(_I32)
        incl_t = jnp.dot(upto_t, sel.T.astype(_BF16), preferred_element_type=_F32).astype(_BF16)
        aff_t = aff.T
        a0 = aff_t.astype(_BF16)
        a1 = (aff_t - a0.astype(_F32)).astype(_BF16)
        a2 = (aff_t - a0.astype(_F32) - a1.astype(_F32)).astype(_BF16)
        for c in range(cap // ns):
            j = slot_id + float(c * ns)
            rj = jnp.sum(jnp.where(cum <= j, 1.0, 0.0), axis=0, keepdims=True)
            hit = row_id == rj
            q = j - jnp.sum(jnp.where(hit, base, 0.0), axis=0, keepdims=True)
            onehot = jnp.where(hit, 1.0, 0.0).astype(_BF16)
            counts = jnp.dot(incl_t, onehot, preferred_element_type=_F32)
            lpos = jnp.sum(jnp.where(counts <= q, 1.0, 0.0), axis=0, keepdims=True)
            arow = (jnp.dot(a0, onehot, preferred_element_type=_F32)
                    + jnp.dot(a1, onehot, preferred_element_type=_F32)
                    + jnp.dot(a2, onehot, preferred_element_type=_F32))
            gate = jnp.sum(jnp.where(lane_id == lpos, arow, 0.0), axis=0, keepdims=True)
            idx_ref[e, pl.ds(c, 1), :] = (rj * float(_LANES) + lpos).astype(_I32)
            gate_ref[e, pl.ds(c, 1), :] = gate
        return carry

    lax.fori_loop(0, n_experts, expert, 0)


def _select(aff_t, cap):
    ne, t = aff_t.shape
    rows = t // _LANES
    ns = _tile(cap, _SLOT_TILE)
    aff2 = aff_t.reshape(ne * rows, _LANES)
    listing = jax.ShapeDtypeStruct((ne, cap // ns, ns), _I32)
    idx, gates, base = pl.pallas_call(
        functools.partial(_select_kernel, cap=cap, n_experts=ne, ns=ns),
        out_shape=[listing, jax.ShapeDtypeStruct(listing.shape, _F32),
                   jax.ShapeDtypeStruct(aff2.shape, _I32)],
        scratch_shapes=[pltpu.VMEM((ne, _LANES), _I32)],
        compiler_params=pltpu.CompilerParams(vmem_limit_bytes=_VMEM_LIMIT),
        name="expert_select",
    )(aff2)
    return idx.reshape(ne, cap), gates.reshape(ne, cap), base.reshape(ne, rows, _LANES)[:, :, 0]


_SUBLANES = 8


def _to_row_tiles(dst_ref, x):
    n = x.shape[0]
    for j in range(x.shape[1] // _LANES):
        dst_ref[pl.ds(j, n, stride=_SUBLANES), :] = x[:, j * _LANES:(j + 1) * _LANES]


def _from_row_tiles(src_ref, n):
    return jnp.concatenate([src_ref[pl.ds(j, n, stride=_SUBLANES), :] for j in range(_SUBLANES)],
                           axis=1)


def _ffn_kernel(idx_ref, nxt_ref, hn_hbm, gate_ref, wg_ref, wu_ref, wd_ref, ye_ref, xbuf, sems,
                *, f_chunk):
    tc = xbuf.shape[1] // _SUBLANES
    n_tiles = pl.num_programs(1)
    step = pl.program_id(0) * n_tiles + pl.program_id(1)
    last = pl.num_programs(0) * n_tiles - 1
    slot = step % 2

    def row_copy(j, t, buf):
        src = hn_hbm.at[pl.ds(pl.multiple_of(t * _SUBLANES, _SUBLANES), _SUBLANES)]
        dst = xbuf.at[buf, pl.ds(pl.multiple_of(j * _SUBLANES, _SUBLANES), _SUBLANES)]
        return pltpu.make_async_copy(src, dst, sems.at[buf])

    def gather(ids_ref, buf, unroll):
        def issue(j, carry):
            row_copy(j, ids_ref[0, 0, j], buf).start()
            return carry
        lax.fori_loop(0, tc, issue, 0, unroll=unroll)

    @pl.when(step == 0)
    def _():
        gather(idx_ref, 0, 8)

    @pl.when(step < last)
    def _():
        gather(nxt_ref, 1 - slot, True)

    def drain(j, carry):
        row_copy(j, 0, slot).wait()
        return carry

    lax.fori_loop(0, tc, drain, 0, unroll=True)
    x = _from_row_tiles(xbuf.at[slot], tc).astype(_BF16)
    d_ff = wg_ref.shape[-1]
    y = jnp.zeros((tc, wd_ref.shape[-1]), _F32)
    for f in range(0, d_ff, f_chunk):
        g = jnp.dot(x, wg_ref[0, :, f:f + f_chunk], preferred_element_type=_F32)
        u = jnp.dot(x, wu_ref[0, :, f:f + f_chunk], preferred_element_type=_F32)
        hid = (g * jax.nn.sigmoid(g) * u).astype(_BF16)
        y = y + jnp.dot(hid, wd_ref[0, f:f + f_chunk, :], preferred_element_type=_F32)
    _to_row_tiles(ye_ref, y * gate_ref[0])


def _expert_ffn(hn, idx, gates, wg, wu, wd, layer):
    ne, cap = idx.shape
    d, d_ff = wg.shape[-2:]
    tc = _tile(cap, _SLOT_TILE)
    nc = cap // tc
    last = ne * nc - 1
    ids = idx.reshape(ne * nc, 1, tc)
    return pl.pallas_call(
        functools.partial(_ffn_kernel, f_chunk=_tile(d_ff, _FF_CHUNK)),
        grid=(ne, nc),
        in_specs=[pl.BlockSpec((1, 1, tc), lambda e, j: (e * nc + j, 0, 0), memory_space=pltpu.SMEM),
                  pl.BlockSpec((1, 1, tc), lambda e, j: (jnp.minimum(e * nc + j + 1, last), 0, 0),
                               memory_space=pltpu.SMEM),
                  pl.BlockSpec(memory_space=pl.ANY),
                  pl.BlockSpec((1, tc, 1), lambda e, j: (e, j, 0)),
                  pl.BlockSpec((None, 1, d, d_ff), lambda e, j: (layer, e, 0, 0)),
                  pl.BlockSpec((None, 1, d, d_ff), lambda e, j: (layer, e, 0, 0)),
                  pl.BlockSpec((None, 1, d_ff, d), lambda e, j: (layer, e, 0, 0))],
        out_specs=pl.BlockSpec((tc * _SUBLANES, _LANES), lambda e, j: (e * nc + j, 0)),
        out_shape=jax.ShapeDtypeStruct((ne * cap * d // _LANES, _LANES), _F32),
        scratch_shapes=[pltpu.VMEM((2, tc * _SUBLANES, _LANES), _F32), pltpu.SemaphoreType.DMA((2,))],
        compiler_params=_cparams("arbitrary", "arbitrary"),
        name="expert_ffn",
    )(ids, ids, hn, gates.reshape(ne, cap, 1), wg, wu, wd)


_ROW_CHUNK = 16
_RMW_GROUP = 8


def _div(x, n):
    if n & (n - 1) == 0:
        return lax.shift_right_logical(x, jnp.int32(n.bit_length() - 1))
    return lax.div(x, jnp.int32(n))


def _combine_kernel(lo_ref, x_ref, ye_hbm, *rest, n_experts, cap, idx_block, final_norm):
    idx_refs, rest = rest[:n_experts], rest[n_experts:]
    if final_norm:
        g_ref, rest = rest[0], rest[1:]
    out_ref, acc, stage, sems = rest
    tb = x_ref.shape[0]
    region = stage.shape[1] // (n_experts * _SUBLANES)
    b = pl.program_id(0)
    slot = b % 2

    def tile_of(row):
        return pl.ds(pl.multiple_of(row * _SUBLANES, _SUBLANES), _SUBLANES)

    def for_each_chunk(blk, buf, fn):
        size = _ROW_CHUNK * _SUBLANES
        for e in range(n_experts):
            lo, hi = lo_ref[e, blk], lo_ref[e, blk + 1]
            c0 = _div(lo, _ROW_CHUNK)

            def one(c, carry):
                src = pl.multiple_of((e * cap + c * _ROW_CHUNK) * _SUBLANES, size)
                dst = pl.multiple_of((e * region + (c - c0) * _ROW_CHUNK) * _SUBLANES, size)
                fn(pltpu.make_async_copy(ye_hbm.at[pl.ds(src, size)],
                                         stage.at[buf, pl.ds(dst, size)], sems.at[buf]))
                return carry

            lax.fori_loop(c0, _div(hi + (_ROW_CHUNK - 1), _ROW_CHUNK), one, 0)

    @pl.when(b == 0)
    def _():
        for_each_chunk(0, 0, lambda cp: cp.start())

    @pl.when(b + 1 < pl.num_programs(0))
    def _():
        for_each_chunk(b + 1, 1 - slot, lambda cp: cp.start())

    _to_row_tiles(acc, x_ref[...])
    for_each_chunk(b, slot, lambda cp: cp.wait())
    for e in range(n_experts):
        lo, hi = lo_ref[e, b], lo_ref[e, b + 1]
        first = _div(lo, idx_block) * idx_block
        row0 = e * region - _div(lo, _ROW_CHUNK) * _ROW_CHUNK
        ids = idx_refs[e]

        def token(j):
            return ids[0, 0, 0, j - first] - b * tb

        def add_rows(j0, count):
            toks = [tile_of(token(j0 + i)) for i in range(count)]
            sums = [acc[toks[i], :] + stage[slot, tile_of(row0 + j0 + i), :] for i in range(count)]
            for i in range(count):
                acc[toks[i], :] = sums[i]

        def group(g, carry):
            add_rows(lo + g * _RMW_GROUP, _RMW_GROUP)
            return carry

        def single(j, carry):
            add_rows(j, 1)
            return carry

        n_groups = _div(hi - lo, _RMW_GROUP)
        lax.fori_loop(0, n_groups, group, 0)
        lax.fori_loop(lo + n_groups * _RMW_GROUP, hi, single, 0)

    y = _from_row_tiles(acc, tb)
    out_ref[...] = _rmsnorm(y, g_ref[...]) if final_norm else y


def _combine(x1, ye, idx, before, final_g=None):
    t, d = x1.shape
    ne, cap = idx.shape
    assert d == _SUBLANES * _LANES
    tb = _tile(t, _COMBINE_BLOCK)
    nb = t // tb
    assert cap % _ROW_CHUNK == 0
    idx_block = min(tb, cap)
    n_ib = cap // idx_block
    lo = jnp.concatenate([before[:, ::tb // _LANES], jnp.full((ne, 1), cap, _I32)], axis=1)
    idx3 = idx.reshape(ne, n_ib, idx_block)
    windows = jnp.concatenate([idx3, jnp.concatenate([idx3[:, 1:], idx3[:, -1:]], axis=1)], axis=2)
    windows = windows.reshape(ne, n_ib, 1, 2 * idx_block)

    def idx_spec(e):
        return pl.BlockSpec((1, 1, 1, 2 * idx_block),
                            lambda b, lo_ref: (e, _div(lo_ref[e, b], idx_block), 0, 0),
                            memory_space=pltpu.SMEM)

    region = tb + _ROW_CHUNK
    row = pl.BlockSpec((tb, d), lambda b, lo_ref: (b, 0))
    norm_specs, norm_args = [], []
    if final_g is not None:
        norm_specs, norm_args = [pl.BlockSpec((1, d), lambda b, lo_ref: (0, 0))], [final_g.reshape(1, d)]
    grid_spec = pltpu.PrefetchScalarGridSpec(
        num_scalar_prefetch=1,
        grid=(nb,),
        in_specs=[row, pl.BlockSpec(memory_space=pl.ANY)] + [idx_spec(e) for e in range(ne)]
                 + norm_specs,
        out_specs=row,
        scratch_shapes=[pltpu.VMEM((tb * _SUBLANES, _LANES), _F32),
                        pltpu.VMEM((2, ne * region * _SUBLANES, _LANES), _F32),
                        pltpu.SemaphoreType.DMA((2,))])
    return pl.pallas_call(
        functools.partial(_combine_kernel, n_experts=ne, cap=cap, idx_block=idx_block,
                          final_norm=final_g is not None),
        grid_spec=grid_spec,
        out_shape=jax.ShapeDtypeStruct((t, d), _F32),
        compiler_params=_cparams("arbitrary"),
        name="moe_combine",
    )(lo, x1, ye, *([windows] * ne), *norm_args)


def _moe(x1, hn, aff_t, wg, wu, wd, layer, final_g):
    t, d = x1.shape
    ne = aff_t.shape[0]
    cap = max(1, _CAPACITY_FACTOR * t // ne)
    idx, gates, before = _select(aff_t, cap)
    ye = _expert_ffn(hn, idx, gates, wg, wu, wd, layer)
    return _combine(x1, ye, idx, before, final_g)


def _trunk(x, p):
    batch, seq, d = x.shape
    x = x.reshape(batch * seq, d)
    depth = p["ffn_norm_g"].shape[0]
    for i in range(depth):
        j = i // _N_MIXERS
        if i % _N_MIXERS == 0:
            lambda_init = 0.8 - 0.6 * float(np.exp(-0.3 * i))
            q, k, v = _diff_qkv(x, p["diff_norm_g"][j], p["diff_w_qkv"][j], batch, seq)
            o = _diff_attn(q, k, v, p["diff_lambda_q1"][j], p["diff_lambda_k1"][j],
                           p["diff_lambda_q2"][j], p["diff_lambda_k2"][j], p["diff_subln_g"][j],
                           batch, seq, lambda_init)
            w_o = p["diff_w_o"][j]
        else:
            q, k, v = _mla_proj(x, p["mla_norm_g"][j], p["mla_w_a"][j], p["mla_q_norm_g"][j],
                                p["mla_w_q_b"][j], p["mla_kv_norm_g"][j], p["mla_w_kv_b"][j],
                                batch, seq)
            o = _mla_attn(q, k, v, batch, seq)
            w_o = p["mla_w_o"][j]
        x1, hn, aff_t = _oproj_router(o, w_o, x, p["ffn_norm_g"][i], p["router_w"][i])
        x = _moe(x1, hn, aff_t, p["w_gate_bf16"], p["w_up_bf16"], p["w_down_bf16"], i,
                 p["final_norm_g"] if i == depth - 1 else None)
    return x.reshape(batch, seq, d)


def kernel(x_prompt, x_sample, diff_norm_g, diff_w_qkv, diff_lambda_q1, diff_lambda_k1, diff_lambda_q2,
           diff_lambda_k2, diff_subln_g, diff_w_o, mla_norm_g, mla_w_a, mla_q_norm_g, mla_w_q_b,
           mla_kv_norm_g, mla_w_kv_b, mla_w_o, ffn_norm_g, router_w, w_gate, w_up, w_down, final_norm_g):
    p = dict(diff_norm_g=diff_norm_g, diff_w_qkv=diff_w_qkv, diff_lambda_q1=diff_lambda_q1,
             diff_lambda_k1=diff_lambda_k1, diff_lambda_q2=diff_lambda_q2, diff_lambda_k2=diff_lambda_k2,
             diff_subln_g=diff_subln_g, diff_w_o=diff_w_o, mla_norm_g=mla_norm_g, mla_w_a=mla_w_a,
             mla_q_norm_g=mla_q_norm_g, mla_w_q_b=mla_w_q_b, mla_kv_norm_g=mla_kv_norm_g,
             mla_w_kv_b=mla_w_kv_b, mla_w_o=mla_w_o, ffn_norm_g=ffn_norm_g, router_w=router_w,
             final_norm_g=final_norm_g, w_gate_bf16=w_gate.astype(_BF16), w_up_bf16=w_up.astype(_BF16),
             w_down_bf16=w_down.astype(_BF16))
    return _trunk(x_prompt, p), _trunk(x_sample, p)
```

```python
import functools

import numpy as np
import jax
import jax.numpy as jnp
from jax import lax
from jax.experimental import pallas as pl
from jax.experimental.pallas import tpu as pltpu

_F32, _BF16, _I32 = jnp.float32, jnp.bfloat16, jnp.int32
_EPS = 1e-6
_ROPE_THETA = 500000.0
_LANES = 128
_NEG = -1e30
_VMEM_LIMIT = 56 * 1024 * 1024

_DIFF_HEADS, _DIFF_HEAD_DIM, _DIFF_ROT = 8, 64, 16
_MLA_HEADS, _MLA_NOPE, _MLA_ROPE, _MLA_V = 16, 64, 32, 64
_MLA_Q_RANK, _MLA_KV_RANK = 384, 256
_N_EXPERTS, _CAPACITY_FACTOR = 16, 2
_N_MIXERS = 2


_TOKEN_TILE = 512
_QUERY_STREAM = 256
_STREAMS_PER_HEAD = 2
_SLOT_TILE = 512
_FF_CHUNK = 512
_COMBINE_BLOCK = 256


def _cparams(*sem):
    return pltpu.CompilerParams(dimension_semantics=sem, vmem_limit_bytes=_VMEM_LIMIT)


def _tile(n, pref):
    t = min(n, pref)
    assert n % t == 0, (n, pref)
    return t


def _rmsnorm(x, g):
    return x * lax.rsqrt(jnp.mean(x * x, axis=-1, keepdims=True) + _EPS) * g


def _rope_lanes(y, c, s_up, s_dn, half):
    return y * c + pltpu.roll(y, _LANES - half, 1) * s_up + pltpu.roll(y, half, 1) * s_dn


def _rope_tables(seq, group, start, rot):
    half = rot // 2
    pos = jnp.arange(seq, dtype=_F32)
    inv = jnp.float32(_ROPE_THETA) ** (-jnp.arange(0, rot, 2, dtype=_F32) / rot)
    ang = pos[:, None] * inv[None, :]
    cos, sin = jnp.cos(ang), jnp.sin(ang)
    j = (np.arange(_LANES) % group) - start
    first = (j >= 0) & (j < half)
    second = (j >= half) & (j < rot)
    f = np.where(first, j, np.where(second, j - half, 0))
    c = jnp.where(first | second, cos[:, f], 1.0)
    s_up = jnp.where(first, -sin[:, f], 0.0)
    s_dn = jnp.where(second, sin[:, f], 0.0)
    return c, s_up, s_dn


_LOG2E = 1.4426950408889634
_ONES_ROWS = 16
_CHUNK_UNROLL = 2


def _key_chunk(seq):
    return _tile(seq, _TOKEN_TILE)


def _query_tiles(seq):
    tqs = _tile(seq, _QUERY_STREAM)
    return _tile(seq, _STREAMS_PER_HEAD * tqs), tqs


def _flash_scratch(n, tq, tk, dv):
    return [pltpu.VMEM((n, tk, tq), _F32), pltpu.VMEM((n, tk, tq), _BF16),
            pltpu.VMEM((n, dv + _ONES_ROWS, tq), _F32)]


def _flash_multi(qs, k_refs, vt_ref, s_scr, p_scr, acc_scr):
    n, tk, tq = s_scr.shape
    n_chunks = vt_ref.shape[0]
    dv = vt_ref.shape[1] - _ONES_ROWS

    def scores(j):
        out = []
        for c in range(n):
            ks = k_refs[c][pl.ds(pl.multiple_of(j * tk, tk), tk), :]
            s = lax.dot_general(ks, qs[c], (((1,), (1,)), ((), ())), preferred_element_type=_F32)
            s_scr[c] = s
            out.append(jnp.max(s, axis=0, keepdims=True))
        return out

    def softmax(smax, st):
        out = []
        for c in range(n):
            m_new = jnp.maximum(st[c][0], smax[c])
            alpha = jnp.exp2(st[c][0] - m_new)
            p_scr[c] = jnp.exp2(s_scr[c] - m_new).astype(_BF16)
            out.append((m_new, alpha))
        return out

    def accumulate(st, j):
        for c in range(n):
            acc_scr[c] = st[c][1] * acc_scr[c] + jnp.dot(vt_ref[j], p_scr[c],
                                                         preferred_element_type=_F32)

    acc_scr[...] = jnp.zeros(acc_scr.shape, _F32)
    st = [(jnp.full((1, tq), _NEG, _F32), None)] * n
    st = softmax(scores(0), st)
    if n_chunks > 1:
        def body(j, carry):
            smax, st = carry
            accumulate(st, j - 1)
            st = softmax(smax, st)
            return scores(j + 1), st

        unroll = _CHUNK_UNROLL if n_chunks - 2 >= 2 * _CHUNK_UNROLL else 1
        smax, st = lax.fori_loop(1, n_chunks - 1, body, (scores(1), st), unroll=unroll)
        accumulate(st, n_chunks - 2)
        st = softmax(smax, st)
    accumulate(st, n_chunks - 1)
    return [(acc_scr[c, :dv, :] * (1.0 / acc_scr[c, dv:dv + 1, :])).T for c in range(n)]


_NT = (((1,), (1,)), ((), ()))


def _store_vt(vt_ref, vt, heads):
    ones = jnp.ones((_ONES_ROWS, vt.shape[1]), _BF16)
    for j in range(heads):
        vt_ref[0, j, 0, :_LANES, :] = vt[j * _LANES:(j + 1) * _LANES, :].astype(_BF16)
        vt_ref[0, j, 0, _LANES:, :] = ones


def _diff_qkv_kernel(x_ref, g_ref, w_ref, wvt_ref, c_ref, su_ref, sd_ref, q_ref, k_ref, vt_ref,
                     *, scale):
    h = _rmsnorm(x_ref[...], g_ref[...]).astype(_BF16)
    y = jnp.dot(h, w_ref[...], preferred_element_type=_F32)
    d = q_ref.shape[-1]
    c, su, sd = c_ref[...], su_ref[...], sd_ref[...]
    half = _DIFF_ROT // 2
    for j in range(d // _LANES):
        lo, hi = j * _LANES, (j + 1) * _LANES
        q_ref[:, lo:hi] = (_rope_lanes(y[:, lo:hi], c, su, sd, half) * scale).astype(_BF16)
        k_ref[:, lo:hi] = _rope_lanes(y[:, d + lo:d + hi], c, su, sd, half).astype(_BF16)
    _store_vt(vt_ref, lax.dot_general(wvt_ref[...], h, _NT, preferred_element_type=_F32), _DIFF_HEADS)


def _vt_shape_spec(batch, heads, seq, tm):
    nseq = seq // tm
    rows = _LANES + _ONES_ROWS
    return (jax.ShapeDtypeStruct((batch, heads, nseq, rows, tm), _BF16),
            pl.BlockSpec((1, heads, 1, rows, tm), lambda i: (i // nseq, 0, i % nseq, 0, 0)))


def _diff_qkv(x, g, w, batch, seq):
    t, d = x.shape
    tm = _key_chunk(seq)
    c, su, sd = _rope_tables(seq, _DIFF_HEAD_DIM, 0, _DIFF_ROT)
    nseq = seq // tm
    tab = pl.BlockSpec((tm, _LANES), lambda i: (i % nseq, 0))
    row = pl.BlockSpec((tm, d), lambda i: (i, 0))
    out = jax.ShapeDtypeStruct((t, d), _BF16)
    vt_shape, vt_spec = _vt_shape_spec(batch, _DIFF_HEADS, seq, tm)
    return pl.pallas_call(
        functools.partial(_diff_qkv_kernel, scale=_DIFF_HEAD_DIM ** -0.5 * _LOG2E),
        grid=(t // tm,),
        in_specs=[row, pl.BlockSpec((1, d), lambda i: (0, 0)),
                  pl.BlockSpec((d, 2 * d), lambda i: (0, 0)),
                  pl.BlockSpec((d, d), lambda i: (0, 0)), tab, tab, tab],
        out_specs=[row, row, vt_spec],
        out_shape=[out, out, vt_shape],
        compiler_params=_cparams("parallel"),
        name="diff_qkv",
    )(x, g.reshape(1, d), w[:, :2 * d].astype(_BF16), w[:, 2 * d:].T.astype(_BF16), c, su, sd)


def _diff_attn_kernel(q_ref, k_ref, v_ref, lq1_ref, lk1_ref, lq2_ref, lk2_ref, g_ref, o_ref,
                      s_scr, p_scr, acc_scr, *, lambda_init):
    tqs = s_scr.shape[-1]
    qs = []
    for r in range(q_ref.shape[0] // tqs):
        q = q_ref[r * tqs:(r + 1) * tqs, :]
        lane = lax.broadcasted_iota(_I32, q.shape, 1)
        zero = jnp.zeros_like(q)
        qs += [jnp.where(lane < _DIFF_HEAD_DIM, q, zero), jnp.where(lane >= _DIFF_HEAD_DIM, q, zero)]
    outs = _flash_multi(qs, [k_ref] * len(qs), v_ref, s_scr, p_scr, acc_scr)
    lam = (jnp.exp(jnp.sum(lq1_ref[...] * lk1_ref[...], axis=-1, keepdims=True))
           - jnp.exp(jnp.sum(lq2_ref[...] * lk2_ref[...], axis=-1, keepdims=True)) + lambda_init)
    for r in range(len(qs) // 2):
        o = outs[2 * r] - lam * outs[2 * r + 1]
        o = _rmsnorm(o, g_ref[...]) * (1.0 - lambda_init)
        o_ref[r * tqs:(r + 1) * tqs, :] = o.astype(_BF16)


def _vt_block(vt):
    return pl.BlockSpec((None, None) + vt.shape[2:], lambda b, h, i: (b, h, 0, 0, 0))


def _diff_attn(q, k, vt, lq1, lk1, lq2, lk2, subln_g, batch, seq, lambda_init):
    t, d = q.shape
    tq, tqs = _query_tiles(seq)
    tk = vt.shape[-1]
    nq = seq // tq
    qspec = pl.BlockSpec((tq, _LANES), lambda b, h, i: (b * nq + i, h))
    kspec = pl.BlockSpec((seq, _LANES), lambda b, h, i: (b, h))
    small = lambda n: pl.BlockSpec((1, n), lambda b, h, i: (0, 0))
    hd = _DIFF_HEAD_DIM
    return pl.pallas_call(
        functools.partial(_diff_attn_kernel, lambda_init=lambda_init),
        grid=(batch, _DIFF_HEADS, nq),
        in_specs=[qspec, kspec, _vt_block(vt), small(hd), small(hd), small(hd), small(hd),
                  small(2 * hd)],
        out_specs=qspec,
        out_shape=jax.ShapeDtypeStruct((t, d), _BF16),
        scratch_shapes=_flash_scratch(2 * (tq // tqs), tqs, tk, _LANES),
        compiler_params=_cparams("parallel", "parallel", "arbitrary"),
        name="diff_attn",
    )(q, k, vt, lq1.reshape(1, hd), lk1.reshape(1, hd), lq2.reshape(1, hd), lk2.reshape(1, hd),
      subln_g.reshape(1, 2 * hd))


def _mla_proj_kernel(x_ref, g_ref, wa_ref, gq_ref, wq_ref, gkv_ref, wk_ref, wvt_ref, c_ref, su_ref,
                     sd_ref, q_ref, k_ref, vt_ref, *, scale):
    h = _rmsnorm(x_ref[...], g_ref[...]).astype(_BF16)
    a = jnp.dot(h, wa_ref[...], preferred_element_type=_F32)
    c, su, sd = c_ref[...], su_ref[...], sd_ref[...]
    half = _MLA_ROPE // 2
    kv_lo = _MLA_Q_RANK + _MLA_KV_RANK
    cq = _rmsnorm(a[:, :_MLA_Q_RANK], gq_ref[...]).astype(_BF16)
    ckv = _rmsnorm(a[:, _MLA_Q_RANK:kv_lo], gkv_ref[...]).astype(_BF16)
    k_rope = _rope_lanes(a[:, kv_lo:kv_lo + _LANES], c, su, sd, half)
    qf = jnp.dot(cq, wq_ref[...], preferred_element_type=_F32)
    kf = jnp.dot(ckv, wk_ref[...], preferred_element_type=_F32)
    for j in range(_MLA_HEADS):
        lo, hi = j * _LANES, (j + 1) * _LANES
        q_ref[:, lo:hi] = (_rope_lanes(qf[:, lo:hi], c, su, sd, half) * scale).astype(_BF16)
        k_ref[:, lo:hi] = (kf[:, lo:hi] + k_rope).astype(_BF16)
    _store_vt(vt_ref, lax.dot_general(wvt_ref[...], ckv, _NT, preferred_element_type=_F32),
              _MLA_HEADS * _MLA_V // _LANES)


def _mla_weights(w_a, w_q_b, w_kv_b):
    d = w_a.shape[0]
    kv_lo = _MLA_Q_RANK + _MLA_KV_RANK
    z = lambda *s: jnp.zeros(s, _F32)
    pad = _LANES - _MLA_NOPE - _MLA_ROPE
    wa = jnp.concatenate([w_a[:, :kv_lo], z(d, _MLA_NOPE), w_a[:, kv_lo:], z(d, pad)], axis=1)
    wq = w_q_b.reshape(_MLA_Q_RANK, _MLA_HEADS, _MLA_NOPE + _MLA_ROPE)
    wq = jnp.concatenate([wq, z(_MLA_Q_RANK, _MLA_HEADS, pad)], axis=2)
    wkv = w_kv_b.reshape(_MLA_KV_RANK, _MLA_HEADS, _MLA_NOPE + _MLA_V)
    wk = jnp.concatenate([wkv[:, :, :_MLA_NOPE], z(_MLA_KV_RANK, _MLA_HEADS, _LANES - _MLA_NOPE)], axis=2)
    wvt = wkv[:, :, _MLA_NOPE:].reshape(_MLA_KV_RANK, -1).T
    return (wa.astype(_BF16), wq.reshape(_MLA_Q_RANK, -1).astype(_BF16),
            wk.reshape(_MLA_KV_RANK, -1).astype(_BF16), wvt.astype(_BF16))


def _mla_proj(x, g, w_a, gq, w_q_b, gkv, w_kv_b, batch, seq):
    t, d = x.shape
    tm = _key_chunk(seq)
    nseq = seq // tm
    c, su, sd = _rope_tables(seq, _LANES, _MLA_NOPE, _MLA_ROPE)
    wa, wq, wk, wvt = _mla_weights(w_a, w_q_b, w_kv_b)
    hq = _MLA_HEADS * _LANES
    tab = pl.BlockSpec((tm, _LANES), lambda i: (i % nseq, 0))
    full = lambda a: pl.BlockSpec(a.shape, lambda i: (0, 0))
    row = lambda n: pl.BlockSpec((tm, n), lambda i: (i, 0))
    g, gq, gkv = g.reshape(1, -1), gq.reshape(1, -1), gkv.reshape(1, -1)
    scale = (_MLA_NOPE + _MLA_ROPE) ** -0.5 * _LOG2E
    vt_shape, vt_spec = _vt_shape_spec(batch, _MLA_HEADS * _MLA_V // _LANES, seq, tm)
    return pl.pallas_call(
        functools.partial(_mla_proj_kernel, scale=scale),
        grid=(t // tm,),
        in_specs=[row(d), full(g), full(wa), full(gq), full(wq), full(gkv), full(wk), full(wvt),
                  tab, tab, tab],
        out_specs=[row(hq), row(hq), vt_spec],
        out_shape=[jax.ShapeDtypeStruct((t, hq), _BF16), jax.ShapeDtypeStruct((t, hq), _BF16), vt_shape],
        compiler_params=_cparams("parallel"),
        name="mla_proj",
    )(x, g, wa, gq, wq, gkv, wk, wvt, c, su, sd)


def _mla_attn_kernel(q_ref, ka_ref, kb_ref, v_ref, o_ref, s_scr, p_scr, acc_scr):
    tqs = s_scr.shape[-1]
    n_rows = q_ref.shape[0] // tqs
    qs = []
    for r in range(n_rows):
        qs += [q_ref[r * tqs:(r + 1) * tqs, :_LANES], q_ref[r * tqs:(r + 1) * tqs, _LANES:]]
    outs = _flash_multi(qs, [ka_ref, kb_ref] * n_rows, v_ref, s_scr, p_scr, acc_scr)
    lane = lax.broadcasted_iota(_I32, outs[0].shape, 1)
    for r in range(n_rows):
        o_ref[r * tqs:(r + 1) * tqs, :] = jnp.where(lane < _MLA_V, outs[2 * r],
                                                    outs[2 * r + 1]).astype(_BF16)


def _mla_attn(q, k, vt, batch, seq):
    t = q.shape[0]
    tq, tqs = _query_tiles(seq)
    tk = vt.shape[-1]
    nq = seq // tq
    return pl.pallas_call(
        _mla_attn_kernel,
        grid=(batch, _MLA_HEADS // 2, nq),
        in_specs=[pl.BlockSpec((tq, 2 * _LANES), lambda b, h, i: (b * nq + i, h)),
                  pl.BlockSpec((seq, _LANES), lambda b, h, i: (b, 2 * h)),
                  pl.BlockSpec((seq, _LANES), lambda b, h, i: (b, 2 * h + 1)),
                  _vt_block(vt)],
        out_specs=pl.BlockSpec((tq, _LANES), lambda b, h, i: (b * nq + i, h)),
        out_shape=jax.ShapeDtypeStruct((t, _MLA_HEADS * _MLA_V), _BF16),
        scratch_shapes=_flash_scratch(2 * (tq // tqs), tqs, tk, _LANES),
        compiler_params=_cparams("parallel", "parallel", "arbitrary"),
        name="mla_attn",
    )(q, k, k, vt)


def _oproj_router_kernel(o_ref, w_ref, x_ref, g_ref, rwh_ref, rwl_ref, x1_ref, hn_ref, aff_ref):
    x1 = x_ref[...] + jnp.dot(o_ref[...], w_ref[...], preferred_element_type=_F32)
    x1_ref[...] = x1
    hn = _rmsnorm(x1, g_ref[...])
    _to_row_tiles(hn_ref, hn)
    hh = hn.astype(_BF16)
    hl = (hn - hh.astype(_F32)).astype(_BF16)
    nt = (((1,), (1,)), ((), ()))
    rwh = rwh_ref[...]
    logits = (lax.dot_general(rwh, hh, nt, preferred_element_type=_F32)
              + lax.dot_general(rwh, hl, nt, preferred_element_type=_F32)
              + lax.dot_general(rwl_ref[...], hh, nt, preferred_element_type=_F32))
    e = jnp.exp(logits - jnp.max(logits, axis=0, keepdims=True))
    aff_ref[...] = e / jnp.sum(e, axis=0, keepdims=True)


def _oproj_router(o, w_o, x, g, router_w):
    t, d = x.shape
    tm = _tile(t, _TOKEN_TILE)
    ne = router_w.shape[1]
    rwt = router_w.T
    rwh = rwt.astype(_BF16)
    rwl = (rwt - rwh.astype(_F32)).astype(_BF16)
    row = pl.BlockSpec((tm, d), lambda i: (i, 0))
    full = lambda a: pl.BlockSpec(a.shape, lambda i: (0, 0))
    g = g.reshape(1, d)
    w = w_o.astype(_BF16)
    return pl.pallas_call(
        _oproj_router_kernel,
        grid=(t // tm,),
        in_specs=[row, full(w), row, full(g), full(rwh), full(rwl)],
        out_specs=[row, pl.BlockSpec((tm * _SUBLANES, _LANES), lambda i: (i, 0)),
                   pl.BlockSpec((ne, tm), lambda i: (0, i))],
        out_shape=[jax.ShapeDtypeStruct((t, d), _F32),
                   jax.ShapeDtypeStruct((t * d // _LANES, _LANES), _F32),
                   jax.ShapeDtypeStruct((ne, t), _F32)],
        compiler_params=_cparams("parallel"),
        name="oproj_router",
    )(o, w, x, g, rwh, rwl)


_MAGNITUDE_BITS = 31


def _select_kernel(aff_ref, idx_ref, gate_ref, base_ref, thr_scr, *, cap, n_experts, ns):
    rows = aff_ref.shape[0] // n_experts

    def iota(shape, dim):
        return lax.broadcasted_iota(_I32, shape, dim)

    sq = (_LANES, _LANES)
    upto = (iota(sq, 0) <= iota(sq, 1)).astype(_BF16)
    upto_t = (iota(sq, 1) <= iota(sq, 0)).astype(_BF16)
    above = (iota((rows, rows), 1) < iota((rows, rows), 0)).astype(_BF16)
    capf = jnp.float32(cap)
    row_id = iota((rows, ns), 0).astype(_F32)
    lane_id = iota((_LANES, ns), 0).astype(_F32)
    slot_id = iota((1, ns), 1).astype(_F32)

    def lanes(col):
        return jnp.broadcast_to(col, (rows, _LANES))

    def rows_before(tot):
        return jnp.dot(above, lanes(tot).astype(_BF16), preferred_element_type=_F32)[:, :1]

    def step(i, thrs):
        bit = lax.shift_left(jnp.int32(1),
                             jnp.int32(_MAGNITUDE_BITS - 1) - lax.convert_element_type(i, _I32))
        out = []
        for e in range(n_experts):
            bits = pltpu.bitcast(aff_ref[e * rows:(e + 1) * rows, :], _I32)
            cand = thrs[e] | bit
            count = jnp.sum(jnp.where(bits >= cand, 1.0, 0.0), keepdims=True)
            out.append(jnp.where(count >= capf, cand, thrs[e]))
        return tuple(out)

    thrs = lax.fori_loop(0, _MAGNITUDE_BITS, step,
                         tuple(jnp.zeros((1, 1), _I32) for _ in range(n_experts)))
    for e in range(n_experts):
        thr_scr[e:e + 1, :] = jnp.broadcast_to(thrs[e], (1, _LANES))

    def expert(e, carry):
        r0 = pl.multiple_of(e * rows, rows)
        aff = aff_ref[pl.ds(r0, rows), :]
        bits = pltpu.bitcast(aff, _I32)
        thr = thr_scr[pl.ds(e, 1), :][:, :1]
        gt = bits > thr
        eq = jnp.where(bits == thr, 1.0, 0.0)
        need = capf - jnp.sum(jnp.where(gt, 1.0, 0.0), keepdims=True)
        rank = (rows_before(jnp.sum(eq, axis=-1, keepdims=True))
                + jnp.dot(eq.astype(_BF16), upto, preferred_element_type=_F32) - eq)
        sel = jnp.where(gt | ((eq > 0.0) & (rank < need)), 1.0, 0.0)

        tot = jnp.sum(sel, axis=-1, keepdims=True)
        base = rows_before(tot)
        cum = base + tot
        base_ref[pl.ds(r0, rows), :] = lanes(base).astype(_I32)
        incl_t = jnp.dot(upto_t, sel.T.astype(_BF16), preferred_element_type=_F32).astype(_BF16)
        aff_t = aff.T
        a0 = aff_t.astype(_BF16)
        a1 = (aff_t - a0.astype(_F32)).astype(_BF16)
        a2 = (aff_t - a0.astype(_F32) - a1.astype(_F32)).astype(_BF16)
        for c in range(cap // ns):
            j = slot_id + float(c * ns)
            rj = jnp.sum(jnp.where(cum <= j, 1.0, 0.0), axis=0, keepdims=True)
            hit = row_id == rj
            q = j - jnp.sum(jnp.where(hit, base, 0.0), axis=0, keepdims=True)
            onehot = jnp.where(hit, 1.0, 0.0).astype(_BF16)
            counts = jnp.dot(incl_t, onehot, preferred_element_type=_F32)
            lpos = jnp.sum(jnp.where(counts <= q, 1.0, 0.0), axis=0, keepdims=True)
            arow = (jnp.dot(a0, onehot, preferred_element_type=_F32)
                    + jnp.dot(a1, onehot, preferred_element_type=_F32)
                    + jnp.dot(a2, onehot, preferred_element_type=_F32))
            gate = jnp.sum(jnp.where(lane_id == lpos, arow, 0.0), axis=0, keepdims=True)
            idx_ref[e, pl.ds(c, 1), :] = (rj * float(_LANES) + lpos).astype(_I32)
            gate_ref[e, pl.ds(c, 1), :] = gate
        return carry

    lax.fori_loop(0, n_experts, expert, 0)


def _select(aff_t, cap):
    ne, t = aff_t.shape
    rows = t // _LANES
    ns = _tile(cap, _SLOT_TILE)
    aff2 = aff_t.reshape(ne * rows, _LANES)
    listing = jax.ShapeDtypeStruct((ne, cap // ns, ns), _I32)
    idx, gates, base = pl.pallas_call(
        functools.partial(_select_kernel, cap=cap, n_experts=ne, ns=ns),
        out_shape=[listing, jax.ShapeDtypeStruct(listing.shape, _F32),
                   jax.ShapeDtypeStruct(aff2.shape, _I32)],
        scratch_shapes=[pltpu.VMEM((ne, _LANES), _I32)],
        compiler_params=pltpu.CompilerParams(vmem_limit_bytes=_VMEM_LIMIT),
        name="expert_select",
    )(aff2)
    return idx.reshape(ne, cap), gates.reshape(ne, cap), base.reshape(ne, rows, _LANES)[:, :, 0]


_SUBLANES = 8


def _to_row_tiles(dst_ref, x):
    n = x.shape[0]
    for j in range(x.shape[1] // _LANES):
        dst_ref[pl.ds(j, n, stride=_SUBLANES), :] = x[:, j * _LANES:(j + 1) * _LANES]


def _from_row_tiles(src_ref, n):
    return jnp.concatenate([src_ref[pl.ds(j, n, stride=_SUBLANES), :] for j in range(_SUBLANES)],
                           axis=1)


def _ffn_kernel(idx_ref, nxt_ref, hn_hbm, gate_ref, wg_ref, wu_ref, wd_ref, ye_ref, xbuf, sems,
                *, f_chunk):
    tc = xbuf.shape[1] // _SUBLANES
    n_tiles = pl.num_programs(1)
    step = pl.program_id(0) * n_tiles + pl.program_id(1)
    last = pl.num_programs(0) * n_tiles - 1
    slot = step % 2

    def row_copy(j, t, buf):
        src = hn_hbm.at[pl.ds(pl.multiple_of(t * _SUBLANES, _SUBLANES), _SUBLANES)]
        dst = xbuf.at[buf, pl.ds(pl.multiple_of(j * _SUBLANES, _SUBLANES), _SUBLANES)]
        return pltpu.make_async_copy(src, dst, sems.at[buf])

    def gather(ids_ref, buf, unroll):
        def issue(j, carry):
            row_copy(j, ids_ref[0, 0, j], buf).start()
            return carry
        lax.fori_loop(0, tc, issue, 0, unroll=unroll)

    @pl.when(step == 0)
    def _():
        gather(idx_ref, 0, 8)

    @pl.when(step < last)
    def _():
        gather(nxt_ref, 1 - slot, True)

    def drain(j, carry):
        row_copy(j, 0, slot).wait()
        return carry

    lax.fori_loop(0, tc, drain, 0, unroll=True)
    x = _from_row_tiles(xbuf.at[slot], tc).astype(_BF16)
    d_ff = wg_ref.shape[-1]
    y = jnp.zeros((tc, wd_ref.shape[-1]), _F32)
    for f in range(0, d_ff, f_chunk):
        g = jnp.dot(x, wg_ref[0, :, f:f + f_chunk], preferred_element_type=_F32)
        u = jnp.dot(x, wu_ref[0, :, f:f + f_chunk], preferred_element_type=_F32)
        hid = (g * jax.nn.sigmoid(g) * u).astype(_BF16)
        y = y + jnp.dot(hid, wd_ref[0, f:f + f_chunk, :], preferred_element_type=_F32)
    _to_row_tiles(ye_ref, y * gate_ref[0])


def _expert_ffn(hn, idx, gates, wg, wu, wd, layer):
    ne, cap = idx.shape
    d, d_ff = wg.shape[-2:]
    tc = _tile(cap, _SLOT_TILE)
    nc = cap // tc
    last = ne * nc - 1
    ids = idx.reshape(ne * nc, 1, tc)
    return pl.pallas_call(
        functools.partial(_ffn_kernel, f_chunk=_tile(d_ff, _FF_CHUNK)),
        grid=(ne, nc),
        in_specs=[pl.BlockSpec((1, 1, tc), lambda e, j: (e * nc + j, 0, 0), memory_space=pltpu.SMEM),
                  pl.BlockSpec((1, 1, tc), lambda e, j: (jnp.minimum(e * nc + j + 1, last), 0, 0),
                               memory_space=pltpu.SMEM),
                  pl.BlockSpec(memory_space=pl.ANY),
                  pl.BlockSpec((1, tc, 1), lambda e, j: (e, j, 0)),
                  pl.BlockSpec((None, 1, d, d_ff), lambda e, j: (layer, e, 0, 0)),
                  pl.BlockSpec((None, 1, d, d_ff), lambda e, j: (layer, e, 0, 0)),
                  pl.BlockSpec((None, 1, d_ff, d), lambda e, j: (layer, e, 0, 0))],
        out_specs=pl.BlockSpec((tc * _SUBLANES, _LANES), lambda e, j: (e * nc + j, 0)),
        out_shape=jax.ShapeDtypeStruct((ne * cap * d // _LANES, _LANES), _F32),
        scratch_shapes=[pltpu.VMEM((2, tc * _SUBLANES, _LANES), _F32), pltpu.SemaphoreType.DMA((2,))],
        compiler_params=_cparams("arbitrary", "arbitrary"),
        name="expert_ffn",
    )(ids, ids, hn, gates.reshape(ne, cap, 1), wg, wu, wd)


_ROW_CHUNK = 16
_RMW_GROUP = 8


def _div(x, n):
    if n & (n - 1) == 0:
        return lax.shift_right_logical(x, jnp.int32(n.bit_length() - 1))
    return lax.div(x, jnp.int32(n))


def _combine_kernel(lo_ref, x_ref, ye_hbm, *rest, n_experts, cap, idx_block, final_norm):
    idx_refs, rest = rest[:n_experts], rest[n_experts:]
    if final_norm:
        g_ref, rest = rest[0], rest[1:]
    out_ref, acc, stage, sems = rest
    tb = x_ref.shape[0]
    region = stage.shape[1] // (n_experts * _SUBLANES)
    b = pl.program_id(0)
    slot = b % 2

    def tile_of(row):
        return pl.ds(pl.multiple_of(row * _SUBLANES, _SUBLANES), _SUBLANES)

    def for_each_chunk(e, blk, buf, fn):
        size = _ROW_CHUNK * _SUBLANES
        lo, hi = lo_ref[e, blk], lo_ref[e, blk + 1]
        c0 = _div(lo, _ROW_CHUNK)

        def one(c, carry):
            src = pl.multiple_of((e * cap + c * _ROW_CHUNK) * _SUBLANES, size)
            dst = pl.multiple_of((e * region + (c - c0) * _ROW_CHUNK) * _SUBLANES, size)
            fn(pltpu.make_async_copy(ye_hbm.at[pl.ds(src, size)],
                                     stage.at[buf, pl.ds(dst, size)], sems.at[buf, e]))
            return carry

        lax.fori_loop(c0, _div(hi + (_ROW_CHUNK - 1), _ROW_CHUNK), one, 0)

    def stage_block(blk, buf):
        for e in range(n_experts):
            for_each_chunk(e, blk, buf, lambda cp: cp.start())

    @pl.when(b == 0)
    def _():
        stage_block(0, 0)

    @pl.when(b + 1 < pl.num_programs(0))
    def _():
        stage_block(b + 1, 1 - slot)

    _to_row_tiles(acc, x_ref[...])
    for e in range(n_experts):
        for_each_chunk(e, b, slot, lambda cp: cp.wait())
        lo, hi = lo_ref[e, b], lo_ref[e, b + 1]
        first = _div(lo, idx_block) * idx_block
        row0 = e * region - _div(lo, _ROW_CHUNK) * _ROW_CHUNK
        ids = idx_refs[e]

        def token(j):
            return ids[0, 0, 0, j - first] - b * tb

        def add_rows(j0, count):
            toks = [tile_of(token(j0 + i)) for i in range(count)]
            sums = [acc[toks[i], :] + stage[slot, tile_of(row0 + j0 + i), :] for i in range(count)]
            for i in range(count):
                acc[toks[i], :] = sums[i]

        def group(g, carry):
            add_rows(lo + g * _RMW_GROUP, _RMW_GROUP)
            return carry

        def single(j, carry):
            add_rows(j, 1)
            return carry

        n_groups = _div(hi - lo, _RMW_GROUP)
        lax.fori_loop(0, n_groups, group, 0)
        lax.fori_loop(lo + n_groups * _RMW_GROUP, hi, single, 0)

    y = _from_row_tiles(acc, tb)
    out_ref[...] = _rmsnorm(y, g_ref[...]) if final_norm else y


def _combine(x1, ye, idx, before, final_g=None):
    t, d = x1.shape
    ne, cap = idx.shape
    assert d == _SUBLANES * _LANES
    tb = _tile(t, _COMBINE_BLOCK)
    nb = t // tb
    assert cap % _ROW_CHUNK == 0
    idx_block = min(tb, cap)
    n_ib = cap // idx_block
    lo = jnp.concatenate([before[:, ::tb // _LANES], jnp.full((ne, 1), cap, _I32)], axis=1)
    idx3 = idx.reshape(ne, n_ib, idx_block)
    windows = jnp.concatenate([idx3, jnp.concatenate([idx3[:, 1:], idx3[:, -1:]], axis=1)], axis=2)
    windows = windows.reshape(ne, n_ib, 1, 2 * idx_block)

    def idx_spec(e):
        return pl.BlockSpec((1, 1, 1, 2 * idx_block),
                            lambda b, lo_ref: (e, _div(lo_ref[e, b], idx_block), 0, 0),
                            memory_space=pltpu.SMEM)

    region = tb + _ROW_CHUNK
    row = pl.BlockSpec((tb, d), lambda b, lo_ref: (b, 0))
    norm_specs, norm_args = [], []
    if final_g is not None:
        norm_specs, norm_args = [pl.BlockSpec((1, d), lambda b, lo_ref: (0, 0))], [final_g.reshape(1, d)]
    grid_spec = pltpu.PrefetchScalarGridSpec(
        num_scalar_prefetch=1,
        grid=(nb,),
        in_specs=[row, pl.BlockSpec(memory_space=pl.ANY)] + [idx_spec(e) for e in range(ne)]
                 + norm_specs,
        out_specs=row,
        scratch_shapes=[pltpu.VMEM((tb * _SUBLANES, _LANES), _F32),
                        pltpu.VMEM((2, ne * region * _SUBLANES, _LANES), _F32),
                        pltpu.SemaphoreType.DMA((2, ne))])
    return pl.pallas_call(
        functools.partial(_combine_kernel, n_experts=ne, cap=cap, idx_block=idx_block,
                          final_norm=final_g is not None),
        grid_spec=grid_spec,
        out_shape=jax.ShapeDtypeStruct((t, d), _F32),
        compiler_params=_cparams("arbitrary"),
        name="moe_combine",
    )(lo, x1, ye, *([windows] * ne), *norm_args)


def _moe(x1, hn, aff_t, wg, wu, wd, layer, final_g):
    t, d = x1.shape
    ne = aff_t.shape[0]
    cap = max(1, _CAPACITY_FACTOR * t // ne)
    idx, gates, before = _select(aff_t, cap)
    ye = _expert_ffn(hn, idx, gates, wg, wu, wd, layer)
    return _combine(x1, ye, idx, before, final_g)


def _trunk(x, p):
    batch, seq, d = x.shape
    x = x.reshape(batch * seq, d)
    depth = p["ffn_norm_g"].shape[0]
    for i in range(depth):
        j = i // _N_MIXERS
        if i % _N_MIXERS == 0:
            lambda_init = 0.8 - 0.6 * float(np.exp(-0.3 * i))
            q, k, v = _diff_qkv(x, p["diff_norm_g"][j], p["diff_w_qkv"][j], batch, seq)
            o = _diff_attn(q, k, v, p["diff_lambda_q1"][j], p["diff_lambda_k1"][j],
                           p["diff_lambda_q2"][j], p["diff_lambda_k2"][j], p["diff_subln_g"][j],
                           batch, seq, lambda_init)
            w_o = p["diff_w_o"][j]
        else:
            q, k, v = _mla_proj(x, p["mla_norm_g"][j], p["mla_w_a"][j], p["mla_q_norm_g"][j],
                                p["mla_w_q_b"][j], p["mla_kv_norm_g"][j], p["mla_w_kv_b"][j],
                                batch, seq)
            o = _mla_attn(q, k, v, batch, seq)
            w_o = p["mla_w_o"][j]
        x1, hn, aff_t = _oproj_router(o, w_o, x, p["ffn_norm_g"][i], p["router_w"][i])
        x = _moe(x1, hn, aff_t, p["w_gate_bf16"], p["w_up_bf16"], p["w_down_bf16"], i,
                 p["final_norm_g"] if i == depth - 1 else None)
    return x.reshape(batch, seq, d)


def kernel(x_prompt, x_sample, diff_norm_g, diff_w_qkv, diff_lambda_q1, diff_lambda_k1, diff_lambda_q2,
           diff_lambda_k2, diff_subln_g, diff_w_o, mla_norm_g, mla_w_a, mla_q_norm_g, mla_w_q_b,
           mla_kv_norm_g, mla_w_kv_b, mla_w_o, ffn_norm_g, router_w, w_gate, w_up, w_down, final_norm_g):
    p = dict(diff_norm_g=diff_norm_g, diff_w_qkv=diff_w_qkv, diff_lambda_q1=diff_lambda_q1,
             diff_lambda_k1=diff_lambda_k1, diff_lambda_q2=diff_lambda_q2, diff_lambda_k2=diff_lambda_k2,
             diff_subln_g=diff_subln_g, diff_w_o=diff_w_o, mla_norm_g=mla_norm_g, mla_w_a=mla_w_a,
             mla_q_norm_g=mla_q_norm_g, mla_w_q_b=mla_w_q_b, mla_kv_norm_g=mla_kv_norm_g,
             mla_w_kv_b=mla_w_kv_b, mla_w_o=mla_w_o, ffn_norm_g=ffn_norm_g, router_w=router_w,
             final_norm_g=final_norm_g, w_gate_bf16=w_gate.astype(_BF16), w_up_bf16=w_up.astype(_BF16),
             w_down_bf16=w_down.astype(_BF16))
    return _trunk(x_prompt, p), _trunk(x_sample, p)
```

```python
import functools

import numpy as np
import jax
import jax.numpy as jnp
from jax import lax
from jax.experimental import pallas as pl
from jax.experimental.pallas import tpu as pltpu

_F32, _BF16, _I32 = jnp.float32, jnp.bfloat16, jnp.int32
_EPS = 1e-6
_ROPE_THETA = 500000.0
_LANES = 128
_NEG = -1e30
_VMEM_LIMIT = 56 * 1024 * 1024

_DIFF_HEADS, _DIFF_HEAD_DIM, _DIFF_ROT = 8, 64, 16
_MLA_HEADS, _MLA_NOPE, _MLA_ROPE, _MLA_V = 16, 64, 32, 64
_MLA_Q_RANK, _MLA_KV_RANK = 384, 256
_CAPACITY_FACTOR = 2
_N_MIXERS = 2


_TOKEN_TILE = 512
_QUERY_STREAM = 256
_STREAMS_PER_HEAD = 2
_SLOT_TILE = 512
_FF_CHUNK = 512
_COMBINE_BLOCK = 256


def _cparams(*sem):
    return pltpu.CompilerParams(dimension_semantics=sem, vmem_limit_bytes=_VMEM_LIMIT)


def _tile(n, pref):
    t = min(n, pref)
    assert n % t == 0, (n, pref)
    return t


def _rmsnorm(x, g):
    return x * lax.rsqrt(jnp.mean(x * x, axis=-1, keepdims=True) + _EPS) * g


def _rope_lanes(y, c, s_up, s_dn, half):
    return y * c + pltpu.roll(y, _LANES - half, 1) * s_up + pltpu.roll(y, half, 1) * s_dn


def _rope_tables(seq, group, start, rot):
    half = rot // 2
    pos = jnp.arange(seq, dtype=_F32)
    inv = jnp.float32(_ROPE_THETA) ** (-jnp.arange(0, rot, 2, dtype=_F32) / rot)
    ang = pos[:, None] * inv[None, :]
    cos, sin = jnp.cos(ang), jnp.sin(ang)
    j = (np.arange(_LANES) % group) - start
    first = (j >= 0) & (j < half)
    second = (j >= half) & (j < rot)
    f = np.where(first, j, np.where(second, j - half, 0))
    c = jnp.where(first | second, cos[:, f], 1.0)
    s_up = jnp.where(first, -sin[:, f], 0.0)
    s_dn = jnp.where(second, sin[:, f], 0.0)
    return c, s_up, s_dn


_LOG2E = 1.4426950408889634
_ONES_ROWS = 16
_CHUNK_UNROLL = 2


def _key_chunk(seq):
    return _tile(seq, _TOKEN_TILE)


def _query_tiles(seq):
    tqs = _tile(seq, _QUERY_STREAM)
    return _tile(seq, _STREAMS_PER_HEAD * tqs), tqs


def _flash_scratch(n, tq, tk, dv):
    return [pltpu.VMEM((n, tk, tq), _F32), pltpu.VMEM((n, tk, tq), _BF16),
            pltpu.VMEM((n, dv + _ONES_ROWS, tq), _F32)]


def _flash_multi(qs, k_refs, vt_ref, s_scr, p_scr, acc_scr):
    n, tk, tq = s_scr.shape
    n_chunks = vt_ref.shape[0]
    dv = vt_ref.shape[1] - _ONES_ROWS

    def scores(j):
        out = []
        for c in range(n):
            ks = k_refs[c][pl.ds(pl.multiple_of(j * tk, tk), tk), :]
            s = lax.dot_general(ks, qs[c], (((1,), (1,)), ((), ())), preferred_element_type=_F32)
            s_scr[c] = s
            out.append(jnp.max(s, axis=0, keepdims=True))
        return out

    def softmax(smax, st):
        out = []
        for c in range(n):
            m_new = jnp.maximum(st[c][0], smax[c])
            alpha = jnp.exp2(st[c][0] - m_new)
            p_scr[c] = jnp.exp2(s_scr[c] - m_new).astype(_BF16)
            out.append((m_new, alpha))
        return out

    def accumulate(st, j):
        for c in range(n):
            acc_scr[c] = st[c][1] * acc_scr[c] + jnp.dot(vt_ref[j], p_scr[c],
                                                         preferred_element_type=_F32)

    acc_scr[...] = jnp.zeros(acc_scr.shape, _F32)
    st = [(jnp.full((1, tq), _NEG, _F32), None)] * n
    st = softmax(scores(0), st)
    if n_chunks > 1:
        def body(j, carry):
            smax, st = carry
            accumulate(st, j - 1)
            st = softmax(smax, st)
            return scores(j + 1), st

        unroll = _CHUNK_UNROLL if n_chunks - 2 >= 2 * _CHUNK_UNROLL else 1
        smax, st = lax.fori_loop(1, n_chunks - 1, body, (scores(1), st), unroll=unroll)
        accumulate(st, n_chunks - 2)
        st = softmax(smax, st)
    accumulate(st, n_chunks - 1)
    return [(acc_scr[c, :dv, :] * (1.0 / acc_scr[c, dv:dv + 1, :])).T for c in range(n)]


_NT = (((1,), (1,)), ((), ()))


def _store_vt(vt_ref, vt, heads):
    ones = jnp.ones((_ONES_ROWS, vt.shape[1]), _BF16)
    for j in range(heads):
        vt_ref[0, j, 0, :_LANES, :] = vt[j * _LANES:(j + 1) * _LANES, :].astype(_BF16)
        vt_ref[0, j, 0, _LANES:, :] = ones


def _diff_qkv_kernel(x_ref, g_ref, w_ref, wvt_ref, c_ref, su_ref, sd_ref, q_ref, k_ref, vt_ref,
                     *, scale):
    h = _rmsnorm(x_ref[...], g_ref[...]).astype(_BF16)
    y = jnp.dot(h, w_ref[...], preferred_element_type=_F32)
    d = q_ref.shape[-1]
    c, su, sd = c_ref[...], su_ref[...], sd_ref[...]
    half = _DIFF_ROT // 2
    for j in range(d // _LANES):
        lo, hi = j * _LANES, (j + 1) * _LANES
        q_ref[:, lo:hi] = (_rope_lanes(y[:, lo:hi], c, su, sd, half) * scale).astype(_BF16)
        k_ref[:, lo:hi] = _rope_lanes(y[:, d + lo:d + hi], c, su, sd, half).astype(_BF16)
    _store_vt(vt_ref, lax.dot_general(wvt_ref[...], h, _NT, preferred_element_type=_F32), _DIFF_HEADS)


def _vt_shape_spec(batch, heads, seq, tm):
    nseq = seq // tm
    rows = _LANES + _ONES_ROWS
    return (jax.ShapeDtypeStruct((batch, heads, nseq, rows, tm), _BF16),
            pl.BlockSpec((1, heads, 1, rows, tm), lambda i: (i // nseq, 0, i % nseq, 0, 0)))


def _diff_qkv(x, g, w, batch, seq):
    t, d = x.shape
    tm = _key_chunk(seq)
    c, su, sd = _rope_tables(seq, _DIFF_HEAD_DIM, 0, _DIFF_ROT)
    nseq = seq // tm
    tab = pl.BlockSpec((tm, _LANES), lambda i: (i % nseq, 0))
    row = pl.BlockSpec((tm, d), lambda i: (i, 0))
    out = jax.ShapeDtypeStruct((t, d), _BF16)
    vt_shape, vt_spec = _vt_shape_spec(batch, _DIFF_HEADS, seq, tm)
    return pl.pallas_call(
        functools.partial(_diff_qkv_kernel, scale=_DIFF_HEAD_DIM ** -0.5 * _LOG2E),
        grid=(t // tm,),
        in_specs=[row, pl.BlockSpec((1, d), lambda i: (0, 0)),
                  pl.BlockSpec((d, 2 * d), lambda i: (0, 0)),
                  pl.BlockSpec((d, d), lambda i: (0, 0)), tab, tab, tab],
        out_specs=[row, row, vt_spec],
        out_shape=[out, out, vt_shape],
        compiler_params=_cparams("parallel"),
        name="diff_qkv",
    )(x, g.reshape(1, d), w[:, :2 * d].astype(_BF16), w[:, 2 * d:].T.astype(_BF16), c, su, sd)


def _diff_attn_kernel(q_ref, k_ref, v_ref, lq1_ref, lk1_ref, lq2_ref, lk2_ref, g_ref, o_ref,
                      s_scr, p_scr, acc_scr, *, lambda_init):
    tqs = s_scr.shape[-1]
    qs = []
    for r in range(q_ref.shape[0] // tqs):
        q = q_ref[r * tqs:(r + 1) * tqs, :]
        lane = lax.broadcasted_iota(_I32, q.shape, 1)
        zero = jnp.zeros_like(q)
        qs += [jnp.where(lane < _DIFF_HEAD_DIM, q, zero), jnp.where(lane >= _DIFF_HEAD_DIM, q, zero)]
    outs = _flash_multi(qs, [k_ref] * len(qs), v_ref, s_scr, p_scr, acc_scr)
    lam = (jnp.exp(jnp.sum(lq1_ref[...] * lk1_ref[...], axis=-1, keepdims=True))
           - jnp.exp(jnp.sum(lq2_ref[...] * lk2_ref[...], axis=-1, keepdims=True)) + lambda_init)
    for r in range(len(qs) // 2):
        o = outs[2 * r] - lam * outs[2 * r + 1]
        o = _rmsnorm(o, g_ref[...]) * (1.0 - lambda_init)
        o_ref[r * tqs:(r + 1) * tqs, :] = o.astype(_BF16)


def _vt_block(vt):
    return pl.BlockSpec((None, None) + vt.shape[2:], lambda b, h, i: (b, h, 0, 0, 0))


def _diff_attn(q, k, vt, lq1, lk1, lq2, lk2, subln_g, batch, seq, lambda_init):
    t, d = q.shape
    tq, tqs = _query_tiles(seq)
    tk = vt.shape[-1]
    nq = seq // tq
    qspec = pl.BlockSpec((tq, _LANES), lambda b, h, i: (b * nq + i, h))
    kspec = pl.BlockSpec((seq, _LANES), lambda b, h, i: (b, h))
    small = lambda n: pl.BlockSpec((1, n), lambda b, h, i: (0, 0))
    hd = _DIFF_HEAD_DIM
    return pl.pallas_call(
        functools.partial(_diff_attn_kernel, lambda_init=lambda_init),
        grid=(batch, _DIFF_HEADS, nq),
        in_specs=[qspec, kspec, _vt_block(vt), small(hd), small(hd), small(hd), small(hd),
                  small(2 * hd)],
        out_specs=qspec,
        out_shape=jax.ShapeDtypeStruct((t, d), _BF16),
        scratch_shapes=_flash_scratch(2 * (tq // tqs), tqs, tk, _LANES),
        compiler_params=_cparams("parallel", "parallel", "arbitrary"),
        name="diff_attn",
    )(q, k, vt, lq1.reshape(1, hd), lk1.reshape(1, hd), lq2.reshape(1, hd), lk2.reshape(1, hd),
      subln_g.reshape(1, 2 * hd))


def _mla_proj_kernel(x_ref, g_ref, wa_ref, gq_ref, wq_ref, gkv_ref, wk_ref, wvt_ref, c_ref, su_ref,
                     sd_ref, q_ref, k_ref, vt_ref, *, scale):
    h = _rmsnorm(x_ref[...], g_ref[...]).astype(_BF16)
    a = jnp.dot(h, wa_ref[...], preferred_element_type=_F32)
    c, su, sd = c_ref[...], su_ref[...], sd_ref[...]
    half = _MLA_ROPE // 2
    kv_lo = _MLA_Q_RANK + _MLA_KV_RANK
    cq = _rmsnorm(a[:, :_MLA_Q_RANK], gq_ref[...]).astype(_BF16)
    ckv = _rmsnorm(a[:, _MLA_Q_RANK:kv_lo], gkv_ref[...]).astype(_BF16)
    k_rope = _rope_lanes(a[:, kv_lo:kv_lo + _LANES], c, su, sd, half)
    qf = jnp.dot(cq, wq_ref[...], preferred_element_type=_F32)
    kf = jnp.dot(ckv, wk_ref[...], preferred_element_type=_F32)
    for j in range(_MLA_HEADS):
        lo, hi = j * _LANES, (j + 1) * _LANES
        q_ref[:, lo:hi] = (_rope_lanes(qf[:, lo:hi], c, su, sd, half) * scale).astype(_BF16)
        k_ref[:, lo:hi] = (kf[:, lo:hi] + k_rope).astype(_BF16)
    _store_vt(vt_ref, lax.dot_general(wvt_ref[...], ckv, _NT, preferred_element_type=_F32),
              _MLA_HEADS * _MLA_V // _LANES)


def _mla_weights(w_a, w_q_b, w_kv_b):
    d = w_a.shape[0]
    kv_lo = _MLA_Q_RANK + _MLA_KV_RANK
    z = lambda *s: jnp.zeros(s, _F32)
    pad = _LANES - _MLA_NOPE - _MLA_ROPE
    wa = jnp.concatenate([w_a[:, :kv_lo], z(d, _MLA_NOPE), w_a[:, kv_lo:], z(d, pad)], axis=1)
    wq = w_q_b.reshape(_MLA_Q_RANK, _MLA_HEADS, _MLA_NOPE + _MLA_ROPE)
    wq = jnp.concatenate([wq, z(_MLA_Q_RANK, _MLA_HEADS, pad)], axis=2)
    wkv = w_kv_b.reshape(_MLA_KV_RANK, _MLA_HEADS, _MLA_NOPE + _MLA_V)
    wk = jnp.concatenate([wkv[:, :, :_MLA_NOPE], z(_MLA_KV_RANK, _MLA_HEADS, _LANES - _MLA_NOPE)], axis=2)
    wvt = wkv[:, :, _MLA_NOPE:].reshape(_MLA_KV_RANK, -1).T
    return (wa.astype(_BF16), wq.reshape(_MLA_Q_RANK, -1).astype(_BF16),
            wk.reshape(_MLA_KV_RANK, -1).astype(_BF16), wvt.astype(_BF16))


def _mla_proj(x, g, w_a, gq, w_q_b, gkv, w_kv_b, batch, seq):
    t, d = x.shape
    tm = _key_chunk(seq)
    nseq = seq // tm
    c, su, sd = _rope_tables(seq, _LANES, _MLA_NOPE, _MLA_ROPE)
    wa, wq, wk, wvt = _mla_weights(w_a, w_q_b, w_kv_b)
    hq = _MLA_HEADS * _LANES
    tab = pl.BlockSpec((tm, _LANES), lambda i: (i % nseq, 0))
    full = lambda a: pl.BlockSpec(a.shape, lambda i: (0, 0))
    row = lambda n: pl.BlockSpec((tm, n), lambda i: (i, 0))
    g, gq, gkv = g.reshape(1, -1), gq.reshape(1, -1), gkv.reshape(1, -1)
    scale = (_MLA_NOPE + _MLA_ROPE) ** -0.5 * _LOG2E
    vt_shape, vt_spec = _vt_shape_spec(batch, _MLA_HEADS * _MLA_V // _LANES, seq, tm)
    return pl.pallas_call(
        functools.partial(_mla_proj_kernel, scale=scale),
        grid=(t // tm,),
        in_specs=[row(d), full(g), full(wa), full(gq), full(wq), full(gkv), full(wk), full(wvt),
                  tab, tab, tab],
        out_specs=[row(hq), row(hq), vt_spec],
        out_shape=[jax.ShapeDtypeStruct((t, hq), _BF16), jax.ShapeDtypeStruct((t, hq), _BF16), vt_shape],
        compiler_params=_cparams("parallel"),
        name="mla_proj",
    )(x, g, wa, gq, wq, gkv, wk, wvt, c, su, sd)


def _mla_attn_kernel(q_ref, ka_ref, kb_ref, v_ref, o_ref, s_scr, p_scr, acc_scr):
    tqs = s_scr.shape[-1]
    n_rows = q_ref.shape[0] // tqs
    qs = []
    for r in range(n_rows):
        qs += [q_ref[r * tqs:(r + 1) * tqs, :_LANES], q_ref[r * tqs:(r + 1) * tqs, _LANES:]]
    outs = _flash_multi(qs, [ka_ref, kb_ref] * n_rows, v_ref, s_scr, p_scr, acc_scr)
    lane = lax.broadcasted_iota(_I32, outs[0].shape, 1)
    for r in range(n_rows):
        o_ref[r * tqs:(r + 1) * tqs, :] = jnp.where(lane < _MLA_V, outs[2 * r],
                                                    outs[2 * r + 1]).astype(_BF16)


def _mla_attn(q, k, vt, batch, seq):
    t = q.shape[0]
    tq, tqs = _query_tiles(seq)
    tk = vt.shape[-1]
    nq = seq // tq
    return pl.pallas_call(
        _mla_attn_kernel,
        grid=(batch, _MLA_HEADS // 2, nq),
        in_specs=[pl.BlockSpec((tq, 2 * _LANES), lambda b, h, i: (b * nq + i, h)),
                  pl.BlockSpec((seq, _LANES), lambda b, h, i: (b, 2 * h)),
                  pl.BlockSpec((seq, _LANES), lambda b, h, i: (b, 2 * h + 1)),
                  _vt_block(vt)],
        out_specs=pl.BlockSpec((tq, _LANES), lambda b, h, i: (b * nq + i, h)),
        out_shape=jax.ShapeDtypeStruct((t, _MLA_HEADS * _MLA_V), _BF16),
        scratch_shapes=_flash_scratch(2 * (tq // tqs), tqs, tk, _LANES),
        compiler_params=_cparams("parallel", "parallel", "arbitrary"),
        name="mla_attn",
    )(q, k, k, vt)


def _oproj_router_kernel(o_ref, w_ref, x_ref, g_ref, rwh_ref, rwl_ref, x1_ref, hn_ref, aff_ref):
    x1 = x_ref[...] + jnp.dot(o_ref[...], w_ref[...], preferred_element_type=_F32)
    x1_ref[...] = x1
    hn = _rmsnorm(x1, g_ref[...])
    _to_row_tiles(hn_ref, hn)
    hh = hn.astype(_BF16)
    hl = (hn - hh.astype(_F32)).astype(_BF16)
    nt = (((1,), (1,)), ((), ()))
    rwh = rwh_ref[...]
    logits = (lax.dot_general(rwh, hh, nt, preferred_element_type=_F32)
              + lax.dot_general(rwh, hl, nt, preferred_element_type=_F32)
              + lax.dot_general(rwl_ref[...], hh, nt, preferred_element_type=_F32))
    e = jnp.exp(logits - jnp.max(logits, axis=0, keepdims=True))
    aff_ref[...] = e / jnp.sum(e, axis=0, keepdims=True)


def _oproj_router(o, w_o, x, g, router_w):
    t, d = x.shape
    tm = _tile(t, _TOKEN_TILE)
    ne = router_w.shape[1]
    rwt = router_w.T
    rwh = rwt.astype(_BF16)
    rwl = (rwt - rwh.astype(_F32)).astype(_BF16)
    row = pl.BlockSpec((tm, d), lambda i: (i, 0))
    full = lambda a: pl.BlockSpec(a.shape, lambda i: (0, 0))
    g = g.reshape(1, d)
    w = w_o.astype(_BF16)
    return pl.pallas_call(
        _oproj_router_kernel,
        grid=(t // tm,),
        in_specs=[row, full(w), row, full(g), full(rwh), full(rwl)],
        out_specs=[row, pl.BlockSpec((tm * _SUBLANES, _LANES), lambda i: (i, 0)),
                   pl.BlockSpec((ne, tm), lambda i: (0, i))],
        out_shape=[jax.ShapeDtypeStruct((t, d), _F32),
                   jax.ShapeDtypeStruct((t * d // _LANES, _LANES), _F32),
                   jax.ShapeDtypeStruct((ne, t), _F32)],
        compiler_params=_cparams("parallel"),
        name="oproj_router",
    )(o, w, x, g, rwh, rwl)


_MAGNITUDE_BITS = 31


def _select_kernel(aff_ref, idx_ref, gate_ref, base_ref, thr_scr, *, cap, n_experts, ns):
    rows = aff_ref.shape[0] // n_experts

    def iota(shape, dim):
        return lax.broadcasted_iota(_I32, shape, dim)

    sq = (_LANES, _LANES)
    upto = (iota(sq, 0) <= iota(sq, 1)).astype(_BF16)
    upto_t = (iota(sq, 1) <= iota(sq, 0)).astype(_BF16)
    above = (iota((rows, rows), 1) < iota((rows, rows), 0)).astype(_BF16)
    capf = jnp.float32(cap)
    row_id = iota((rows, ns), 0).astype(_F32)
    lane_id = iota((_LANES, ns), 0).astype(_F32)
    slot_id = iota((1, ns), 1).astype(_F32)

    def lanes(col):
        return jnp.broadcast_to(col, (rows, _LANES))

    def rows_before(tot):
        return jnp.dot(above, lanes(tot).astype(_BF16), preferred_element_type=_F32)[:, :1]

    def step(i, thrs):
        bit = lax.shift_left(jnp.int32(1),
                             jnp.int32(_MAGNITUDE_BITS - 1) - lax.convert_element_type(i, _I32))
        out = []
        for e in range(n_experts):
            bits = pltpu.bitcast(aff_ref[e * rows:(e + 1) * rows, :], _I32)
            cand = thrs[e] | bit
            count = jnp.sum(jnp.where(bits >= cand, 1.0, 0.0), keepdims=True)
            out.append(jnp.where(count >= capf, cand, thrs[e]))
        return tuple(out)

    thrs = lax.fori_loop(0, _MAGNITUDE_BITS, step,
                         tuple(jnp.zeros((1, 1), _I32) for _ in range(n_experts)))
    for e in range(n_experts):
        thr_scr[e:e + 1, :] = jnp.broadcast_to(thrs[e], (1, _LANES))

    def expert(e, carry):
        r0 = pl.multiple_of(e * rows, rows)
        aff = aff_ref[pl.ds(r0, rows), :]
        bits = pltpu.bitcast(aff, _I32)
        thr = thr_scr[pl.ds(e, 1), :][:, :1]
        gt = bits > thr
        eq = jnp.where(bits == thr, 1.0, 0.0)
        need = capf - jnp.sum(jnp.where(gt, 1.0, 0.0), keepdims=True)
        rank = (rows_before(jnp.sum(eq, axis=-1, keepdims=True))
                + jnp.dot(eq.astype(_BF16), upto, preferred_element_type=_F32) - eq)
        sel = jnp.where(gt | ((eq > 0.0) & (rank < need)), 1.0, 0.0)

        tot = jnp.sum(sel, axis=-1, keepdims=True)
        base = rows_before(tot)
        cum = base + tot
        base_ref[pl.ds(r0, rows), :] = lanes(base).astype(_I32)
        incl_t = jnp.dot(upto_t, sel.T.astype(_BF16), preferred_element_type=_F32).astype(_BF16)
        aff_t = aff.T
        a0 = aff_t.astype(_BF16)
        a1 = (aff_t - a0.astype(_F32)).astype(_BF16)
        a2 = (aff_t - a0.astype(_F32) - a1.astype(_F32)).astype(_BF16)
        for c in range(cap // ns):
            j = slot_id + float(c * ns)
            rj = jnp.sum(jnp.where(cum <= j, 1.0, 0.0), axis=0, keepdims=True)
            hit = row_id == rj
            q = j - jnp.sum(jnp.where(hit, base, 0.0), axis=0, keepdims=True)
            onehot = jnp.where(hit, 1.0, 0.0).astype(_BF16)
            counts = jnp.dot(incl_t, onehot, preferred_element_type=_F32)
            lpos = jnp.sum(jnp.where(counts <= q, 1.0, 0.0), axis=0, keepdims=True)
            arow = (jnp.dot(a0, onehot, preferred_element_type=_F32)
                    + jnp.dot(a1, onehot, preferred_element_type=_F32)
                    + jnp.dot(a2, onehot, preferred_element_type=_F32))
            gate = jnp.sum(jnp.where(lane_id == lpos, arow, 0.0), axis=0, keepdims=True)
            idx_ref[e, pl.ds(c, 1), :] = (rj * float(_LANES) + lpos).astype(_I32)
            gate_ref[e, pl.ds(c, 1), :] = gate
        return carry

    lax.fori_loop(0, n_experts, expert, 0)


def _select(aff_t, cap):
    ne, t = aff_t.shape
    rows = t // _LANES
    ns = _tile(cap, _SLOT_TILE)
    aff2 = aff_t.reshape(ne * rows, _LANES)
    listing = jax.ShapeDtypeStruct((ne, cap // ns, ns), _I32)
    idx, gates, base = pl.pallas_call(
        functools.partial(_select_kernel, cap=cap, n_experts=ne, ns=ns),
        out_shape=[listing, jax.ShapeDtypeStruct(listing.shape, _F32),
                   jax.ShapeDtypeStruct(aff2.shape, _I32)],
        scratch_shapes=[pltpu.VMEM((ne, _LANES), _I32)],
        compiler_params=pltpu.CompilerParams(vmem_limit_bytes=_VMEM_LIMIT),
        name="expert_select",
    )(aff2)
    return idx.reshape(ne, cap), gates.reshape(ne, cap), base.reshape(ne, rows, _LANES)[:, :, 0]


_SUBLANES = 8


def _to_row_tiles(dst_ref, x):
    n = x.shape[0]
    for j in range(x.shape[1] // _LANES):
        dst_ref[pl.ds(j, n, stride=_SUBLANES), :] = x[:, j * _LANES:(j + 1) * _LANES]


def _from_row_tiles(src_ref, n):
    return jnp.concatenate([src_ref[pl.ds(j, n, stride=_SUBLANES), :] for j in range(_SUBLANES)],
                           axis=1)


def _ffn_kernel(idx_ref, nxt_ref, hn_hbm, gate_ref, wg_ref, wu_ref, wd_ref, ye_ref, xbuf, sems,
                *, f_chunk):
    tc = xbuf.shape[1] // _SUBLANES
    n_tiles = pl.num_programs(1)
    step = pl.program_id(0) * n_tiles + pl.program_id(1)
    last = pl.num_programs(0) * n_tiles - 1
    slot = step % 2

    def row_copy(j, t, buf):
        src = hn_hbm.at[pl.ds(pl.multiple_of(t * _SUBLANES, _SUBLANES), _SUBLANES)]
        dst = xbuf.at[buf, pl.ds(pl.multiple_of(j * _SUBLANES, _SUBLANES), _SUBLANES)]
        return pltpu.make_async_copy(src, dst, sems.at[buf])

    def gather(ids_ref, buf, unroll):
        def issue(j, carry):
            row_copy(j, ids_ref[0, 0, j], buf).start()
            return carry
        lax.fori_loop(0, tc, issue, 0, unroll=unroll)

    @pl.when(step == 0)
    def _():
        gather(idx_ref, 0, 8)

    @pl.when(step < last)
    def _():
        gather(nxt_ref, 1 - slot, True)

    def drain(j, carry):
        row_copy(j, 0, slot).wait()
        return carry

    lax.fori_loop(0, tc, drain, 0, unroll=True)
    x = _from_row_tiles(xbuf.at[slot], tc).astype(_BF16)
    d_ff = wg_ref.shape[-1]
    y = jnp.zeros((tc, wd_ref.shape[-1]), _F32)
    for f in range(0, d_ff, f_chunk):
        g = jnp.dot(x, wg_ref[0, :, f:f + f_chunk], preferred_element_type=_F32)
        u = jnp.dot(x, wu_ref[0, :, f:f + f_chunk], preferred_element_type=_F32)
        hid = (g * jax.nn.sigmoid(g) * u).astype(_BF16)
        y = y + jnp.dot(hid, wd_ref[0, f:f + f_chunk, :], preferred_element_type=_F32)
    _to_row_tiles(ye_ref, y * gate_ref[0])


def _expert_ffn(hn, idx, gates, wg, wu, wd, layer):
    ne, cap = idx.shape
    d, d_ff = wg.shape[-2:]
    tc = _tile(cap, _SLOT_TILE)
    nc = cap // tc
    last = ne * nc - 1
    ids = idx.reshape(ne * nc, 1, tc)
    return pl.pallas_call(
        functools.partial(_ffn_kernel, f_chunk=_tile(d_ff, _FF_CHUNK)),
        grid=(ne, nc),
        in_specs=[pl.BlockSpec((1, 1, tc), lambda e, j: (e * nc + j, 0, 0), memory_space=pltpu.SMEM),
                  pl.BlockSpec((1, 1, tc), lambda e, j: (jnp.minimum(e * nc + j + 1, last), 0, 0),
                               memory_space=pltpu.SMEM),
                  pl.BlockSpec(memory_space=pl.ANY),
                  pl.BlockSpec((1, tc, 1), lambda e, j: (e, j, 0)),
                  pl.BlockSpec((None, 1, d, d_ff), lambda e, j: (layer, e, 0, 0)),
                  pl.BlockSpec((None, 1, d, d_ff), lambda e, j: (layer, e, 0, 0)),
                  pl.BlockSpec((None, 1, d_ff, d), lambda e, j: (layer, e, 0, 0))],
        out_specs=pl.BlockSpec((tc * _SUBLANES, _LANES), lambda e, j: (e * nc + j, 0)),
        out_shape=jax.ShapeDtypeStruct((ne * cap * d // _LANES, _LANES), _F32),
        scratch_shapes=[pltpu.VMEM((2, tc * _SUBLANES, _LANES), _F32), pltpu.SemaphoreType.DMA((2,))],
        compiler_params=_cparams("arbitrary", "arbitrary"),
        name="expert_ffn",
    )(ids, ids, hn, gates.reshape(ne, cap, 1), wg, wu, wd)


_ROW_CHUNK = 16
_RMW_GROUP = 8


def _div(x, n):
    if n & (n - 1) == 0:
        return lax.shift_right_logical(x, jnp.int32(n.bit_length() - 1))
    return lax.div(x, jnp.int32(n))


def _combine_kernel(lo_ref, x_ref, ye_hbm, *rest, n_experts, cap, idx_block, final_norm):
    idx_refs, rest = rest[:n_experts], rest[n_experts:]
    if final_norm:
        g_ref, rest = rest[0], rest[1:]
    out_ref, acc, stage, sems = rest
    tb = x_ref.shape[0]
    region = stage.shape[1] // (n_experts * _SUBLANES)
    b = pl.program_id(0)
    slot = b % 2

    def tile_of(row):
        return pl.ds(pl.multiple_of(row * _SUBLANES, _SUBLANES), _SUBLANES)

    def for_each_chunk(blk, buf, fn):
        size = _ROW_CHUNK * _SUBLANES
        for e in range(n_experts):
            lo, hi = lo_ref[e, blk], lo_ref[e, blk + 1]
            c0 = _div(lo, _ROW_CHUNK)

            def one(c, carry):
                src = pl.multiple_of((e * cap + c * _ROW_CHUNK) * _SUBLANES, size)
                dst = pl.multiple_of((e * region + (c - c0) * _ROW_CHUNK) * _SUBLANES, size)
                fn(pltpu.make_async_copy(ye_hbm.at[pl.ds(src, size)],
                                         stage.at[buf, pl.ds(dst, size)], sems.at[buf]))
                return carry

            lax.fori_loop(c0, _div(hi + (_ROW_CHUNK - 1), _ROW_CHUNK), one, 0)

    @pl.when(b == 0)
    def _():
        for_each_chunk(0, 0, lambda cp: cp.start())

    @pl.when(b + 1 < pl.num_programs(0))
    def _():
        for_each_chunk(b + 1, 1 - slot, lambda cp: cp.start())

    _to_row_tiles(acc, x_ref[...])
    for_each_chunk(b, slot, lambda cp: cp.wait())
    for e in range(n_experts):
        lo, hi = lo_ref[e, b], lo_ref[e, b + 1]
        first = _div(lo, idx_block) * idx_block
        row0 = e * region - _div(lo, _ROW_CHUNK) * _ROW_CHUNK
        ids = idx_refs[e]

        def token(j):
            return ids[0, 0, 0, j - first] - b * tb

        def add_rows(j0, count):
            toks = [tile_of(token(j0 + i)) for i in range(count)]
            sums = [acc[toks[i], :] + stage[slot, tile_of(row0 + j0 + i), :] for i in range(count)]
            for i in range(count):
                acc[toks[i], :] = sums[i]

        def group(g, carry):
            add_rows(lo + g * _RMW_GROUP, _RMW_GROUP)
            return carry

        def single(j, carry):
            add_rows(j, 1)
            return carry

        n_groups = _div(hi - lo, _RMW_GROUP)
        lax.fori_loop(0, n_groups, group, 0)
        lax.fori_loop(lo + n_groups * _RMW_GROUP, hi, single, 0)

    y = _from_row_tiles(acc, tb)
    out_ref[...] = _rmsnorm(y, g_ref[...]) if final_norm else y


def _combine(x1, ye, idx, before, final_g=None):
    t, d = x1.shape
    ne, cap = idx.shape
    assert d == _SUBLANES * _LANES
    tb = _tile(t, _COMBINE_BLOCK)
    nb = t // tb
    assert cap % _ROW_CHUNK == 0
    idx_block = min(tb, cap)
    n_ib = cap // idx_block
    lo = jnp.concatenate([before[:, ::tb // _LANES], jnp.full((ne, 1), cap, _I32)], axis=1)
    idx3 = idx.reshape(ne, n_ib, idx_block)
    windows = jnp.concatenate([idx3, jnp.concatenate([idx3[:, 1:], idx3[:, -1:]], axis=1)], axis=2)
    windows = windows.reshape(ne, n_ib, 1, 2 * idx_block)

    def idx_spec(e):
        return pl.BlockSpec((1, 1, 1, 2 * idx_block),
                            lambda b, lo_ref: (e, _div(lo_ref[e, b], idx_block), 0, 0),
                            memory_space=pltpu.SMEM)

    region = tb + _ROW_CHUNK
    row = pl.BlockSpec((tb, d), lambda b, lo_ref: (b, 0))
    norm_specs, norm_args = [], []
    if final_g is not None:
        norm_specs, norm_args = [pl.BlockSpec((1, d), lambda b, lo_ref: (0, 0))], [final_g.reshape(1, d)]
    grid_spec = pltpu.PrefetchScalarGridSpec(
        num_scalar_prefetch=1,
        grid=(nb,),
        in_specs=[row, pl.BlockSpec(memory_space=pl.ANY)] + [idx_spec(e) for e in range(ne)]
                 + norm_specs,
        out_specs=row,
        scratch_shapes=[pltpu.VMEM((tb * _SUBLANES, _LANES), _F32),
                        pltpu.VMEM((2, ne * region * _SUBLANES, _LANES), _F32),
                        pltpu.SemaphoreType.DMA((2,))])
    return pl.pallas_call(
        functools.partial(_combine_kernel, n_experts=ne, cap=cap, idx_block=idx_block,
                          final_norm=final_g is not None),
        grid_spec=grid_spec,
        out_shape=jax.ShapeDtypeStruct((t, d), _F32),
        compiler_params=_cparams("arbitrary"),
        name="moe_combine",
    )(lo, x1, ye, *([windows] * ne), *norm_args)


def _moe(x1, hn, aff_t, wg, wu, wd, layer, final_g):
    t, d = x1.shape
    ne = aff_t.shape[0]
    cap = max(1, _CAPACITY_FACTOR * t // ne)
    idx, gates, before = _select(aff_t, cap)
    ye = _expert_ffn(hn, idx, gates, wg, wu, wd, layer)
    return _combine(x1, ye, idx, before, final_g)


def _trunk(x, p):
    batch, seq, d = x.shape
    x = x.reshape(batch * seq, d)
    depth = p["ffn_norm_g"].shape[0]
    for i in range(depth):
        j = i // _N_MIXERS
        if i % _N_MIXERS == 0:
            lambda_init = 0.8 - 0.6 * float(np.exp(-0.3 * i))
            q, k, v = _diff_qkv(x, p["diff_norm_g"][j], p["diff_w_qkv"][j], batch, seq)
            o = _diff_attn(q, k, v, p["diff_lambda_q1"][j], p["diff_lambda_k1"][j],
                           p["diff_lambda_q2"][j], p["diff_lambda_k2"][j], p["diff_subln_g"][j],
                           batch, seq, lambda_init)
            w_o = p["diff_w_o"][j]
        else:
            q, k, v = _mla_proj(x, p["mla_norm_g"][j], p["mla_w_a"][j], p["mla_q_norm_g"][j],
                                p["mla_w_q_b"][j], p["mla_kv_norm_g"][j], p["mla_w_kv_b"][j],
                                batch, seq)
            o = _mla_attn(q, k, v, batch, seq)
            w_o = p["mla_w_o"][j]
        x1, hn, aff_t = _oproj_router(o, w_o, x, p["ffn_norm_g"][i], p["router_w"][i])
        x = _moe(x1, hn, aff_t, p["w_gate_bf16"], p["w_up_bf16"], p["w_down_bf16"], i,
                 p["final_norm_g"] if i == depth - 1 else None)
    return x.reshape(batch, seq, d)


def kernel(x_prompt, x_sample, diff_norm_g, diff_w_qkv, diff_lambda_q1, diff_lambda_k1, diff_lambda_q2,
           diff_lambda_k2, diff_subln_g, diff_w_o, mla_norm_g, mla_w_a, mla_q_norm_g, mla_w_q_b,
           mla_kv_norm_g, mla_w_kv_b, mla_w_o, ffn_norm_g, router_w, w_gate, w_up, w_down, final_norm_g):
    p = dict(diff_norm_g=diff_norm_g, diff_w_qkv=diff_w_qkv, diff_lambda_q1=diff_lambda_q1,
             diff_lambda_k1=diff_lambda_k1, diff_lambda_q2=diff_lambda_q2, diff_lambda_k2=diff_lambda_k2,
             diff_subln_g=diff_subln_g, diff_w_o=diff_w_o, mla_norm_g=mla_norm_g, mla_w_a=mla_w_a,
             mla_q_norm_g=mla_q_norm_g, mla_w_q_b=mla_w_q_b, mla_kv_norm_g=mla_kv_norm_g,
             mla_w_kv_b=mla_w_kv_b, mla_w_o=mla_w_o, ffn_norm_g=ffn_norm_g, router_w=router_w,
             final_norm_g=final_norm_g, w_gate_bf16=w_gate.astype(_BF16), w_up_bf16=w_up.astype(_BF16),
             w_down_bf16=w_down.astype(_BF16))
    return _trunk(x_prompt, p), _trunk(x_sample, p)
```

```python
import functools

import numpy as np
import jax
import jax.numpy as jnp
from jax import lax
from jax.experimental import pallas as pl
from jax.experimental.pallas import tpu as pltpu

_F32, _BF16, _I32 = jnp.float32, jnp.bfloat16, jnp.int32
_EPS = 1e-6
_ROPE_THETA = 500000.0
_LANES = 128
_NEG = -1e30
_VMEM_LIMIT = 56 * 1024 * 1024

_DIFF_HEADS, _DIFF_HEAD_DIM, _DIFF_ROT = 8, 64, 16
_MLA_HEADS, _MLA_NOPE, _MLA_ROPE, _MLA_V = 16, 64, 32, 64
_MLA_Q_RANK, _MLA_KV_RANK = 384, 256
_CAPACITY_FACTOR = 2
_N_MIXERS = 2


_TOKEN_TILE = 512
_QUERY_STREAM = 256
_STREAMS_PER_HEAD = 2
_PASSES_PER_STEP = 2
_SLOT_TILE = 512
_FF_CHUNK = 512
_COMBINE_BLOCK = 256


def _cparams(*sem):
    return pltpu.CompilerParams(dimension_semantics=sem, vmem_limit_bytes=_VMEM_LIMIT)


def _tile(n, pref):
    t = min(n, pref)
    assert n % t == 0, (n, pref)
    return t


def _rmsnorm(x, g):
    return x * lax.rsqrt(jnp.mean(x * x, axis=-1, keepdims=True) + _EPS) * g


def _rope_lanes(y, c, s_up, s_dn, half):
    return y * c + pltpu.roll(y, _LANES - half, 1) * s_up + pltpu.roll(y, half, 1) * s_dn


def _rope_tables(seq, group, start, rot):
    half = rot // 2
    pos = jnp.arange(seq, dtype=_F32)
    inv = jnp.float32(_ROPE_THETA) ** (-jnp.arange(0, rot, 2, dtype=_F32) / rot)
    ang = pos[:, None] * inv[None, :]
    cos, sin = jnp.cos(ang), jnp.sin(ang)
    j = (np.arange(_LANES) % group) - start
    first = (j >= 0) & (j < half)
    second = (j >= half) & (j < rot)
    f = np.where(first, j, np.where(second, j - half, 0))
    c = jnp.where(first | second, cos[:, f], 1.0)
    s_up = jnp.where(first, -sin[:, f], 0.0)
    s_dn = jnp.where(second, sin[:, f], 0.0)
    return c, s_up, s_dn


_LOG2E = 1.4426950408889634
_ONES_ROWS = 16
_CHUNK_UNROLL = 2


def _key_chunk(seq):
    return _tile(seq, _TOKEN_TILE)


def _query_tiles(seq):
    tqs = _tile(seq, _QUERY_STREAM)
    together = _tile(seq, _STREAMS_PER_HEAD * tqs)
    return _tile(seq, _PASSES_PER_STEP * together), tqs, 2 * (together // tqs)


def _flash_scratch(n, tq, tk, dv):
    return [pltpu.VMEM((n, tk, tq), _F32), pltpu.VMEM((n, tk, tq), _BF16),
            pltpu.VMEM((n, dv + _ONES_ROWS, tq), _F32)]


def _flash_multi(qs, k_refs, vt_ref, s_scr, p_scr, acc_scr):
    n, tk, tq = s_scr.shape
    n_chunks = vt_ref.shape[0]
    dv = vt_ref.shape[1] - _ONES_ROWS

    def scores(j):
        out = []
        for c in range(n):
            ks = k_refs[c][pl.ds(pl.multiple_of(j * tk, tk), tk), :]
            s = lax.dot_general(ks, qs[c], (((1,), (1,)), ((), ())), preferred_element_type=_F32)
            s_scr[c] = s
            out.append(jnp.max(s, axis=0, keepdims=True))
        return out

    def softmax(smax, st):
        out = []
        for c in range(n):
            m_new = jnp.maximum(st[c][0], smax[c])
            alpha = jnp.exp2(st[c][0] - m_new)
            p_scr[c] = jnp.exp2(s_scr[c] - m_new).astype(_BF16)
            out.append((m_new, alpha))
        return out

    def accumulate(st, j):
        for c in range(n):
            acc_scr[c] = st[c][1] * acc_scr[c] + jnp.dot(vt_ref[j], p_scr[c],
                                                         preferred_element_type=_F32)

    acc_scr[...] = jnp.zeros(acc_scr.shape, _F32)
    st = [(jnp.full((1, tq), _NEG, _F32), None)] * n
    st = softmax(scores(0), st)
    if n_chunks > 1:
        def body(j, carry):
            smax, st = carry
            accumulate(st, j - 1)
            st = softmax(smax, st)
            return scores(j + 1), st

        unroll = _CHUNK_UNROLL if n_chunks - 2 >= 2 * _CHUNK_UNROLL else 1
        smax, st = lax.fori_loop(1, n_chunks - 1, body, (scores(1), st), unroll=unroll)
        accumulate(st, n_chunks - 2)
        st = softmax(smax, st)
    accumulate(st, n_chunks - 1)
    return [(acc_scr[c, :dv, :] * (1.0 / acc_scr[c, dv:dv + 1, :])).T for c in range(n)]


_NT = (((1,), (1,)), ((), ()))


def _store_vt(vt_ref, vt, heads):
    ones = jnp.ones((_ONES_ROWS, vt.shape[1]), _BF16)
    for j in range(heads):
        vt_ref[0, j, 0, :_LANES, :] = vt[j * _LANES:(j + 1) * _LANES, :].astype(_BF16)
        vt_ref[0, j, 0, _LANES:, :] = ones


def _diff_qkv_kernel(x_ref, g_ref, w_ref, wvt_ref, c_ref, su_ref, sd_ref, q_ref, k_ref, vt_ref,
                     *, scale):
    h = _rmsnorm(x_ref[...], g_ref[...]).astype(_BF16)
    y = jnp.dot(h, w_ref[...], preferred_element_type=_F32)
    d = q_ref.shape[-1]
    c, su, sd = c_ref[...], su_ref[...], sd_ref[...]
    half = _DIFF_ROT // 2
    for j in range(d // _LANES):
        lo, hi = j * _LANES, (j + 1) * _LANES
        q_ref[:, lo:hi] = (_rope_lanes(y[:, lo:hi], c, su, sd, half) * scale).astype(_BF16)
        k_ref[:, lo:hi] = _rope_lanes(y[:, d + lo:d + hi], c, su, sd, half).astype(_BF16)
    _store_vt(vt_ref, lax.dot_general(wvt_ref[...], h, _NT, preferred_element_type=_F32), _DIFF_HEADS)


def _vt_shape_spec(batch, heads, seq, tm):
    nseq = seq // tm
    rows = _LANES + _ONES_ROWS
    return (jax.ShapeDtypeStruct((batch, heads, nseq, rows, tm), _BF16),
            pl.BlockSpec((1, heads, 1, rows, tm), lambda i: (i // nseq, 0, i % nseq, 0, 0)))


def _diff_qkv(x, g, w, batch, seq):
    t, d = x.shape
    tm = _key_chunk(seq)
    c, su, sd = _rope_tables(seq, _DIFF_HEAD_DIM, 0, _DIFF_ROT)
    nseq = seq // tm
    tab = pl.BlockSpec((tm, _LANES), lambda i: (i % nseq, 0))
    row = pl.BlockSpec((tm, d), lambda i: (i, 0))
    out = jax.ShapeDtypeStruct((t, d), _BF16)
    vt_shape, vt_spec = _vt_shape_spec(batch, _DIFF_HEADS, seq, tm)
    return pl.pallas_call(
        functools.partial(_diff_qkv_kernel, scale=_DIFF_HEAD_DIM ** -0.5 * _LOG2E),
        grid=(t // tm,),
        in_specs=[row, pl.BlockSpec((1, d), lambda i: (0, 0)),
                  pl.BlockSpec((d, 2 * d), lambda i: (0, 0)),
                  pl.BlockSpec((d, d), lambda i: (0, 0)), tab, tab, tab],
        out_specs=[row, row, vt_spec],
        out_shape=[out, out, vt_shape],
        compiler_params=_cparams("parallel"),
        name="diff_qkv",
    )(x, g.reshape(1, d), w[:, :2 * d].astype(_BF16), w[:, 2 * d:].T.astype(_BF16), c, su, sd)


def _diff_attn_kernel(q_ref, k_ref, v_ref, lq1_ref, lk1_ref, lq2_ref, lk2_ref, g_ref, o_ref,
                      s_scr, p_scr, acc_scr, *, lambda_init):
    tqs = s_scr.shape[-1]
    subs = s_scr.shape[0] // 2
    lam = (jnp.exp(jnp.sum(lq1_ref[...] * lk1_ref[...], axis=-1, keepdims=True))
           - jnp.exp(jnp.sum(lq2_ref[...] * lk2_ref[...], axis=-1, keepdims=True)) + lambda_init)
    for base in range(0, q_ref.shape[0], subs * tqs):
        qs = []
        for r in range(subs):
            q = q_ref[base + r * tqs:base + (r + 1) * tqs, :]
            lane = lax.broadcasted_iota(_I32, q.shape, 1)
            zero = jnp.zeros_like(q)
            qs += [jnp.where(lane < _DIFF_HEAD_DIM, q, zero),
                   jnp.where(lane >= _DIFF_HEAD_DIM, q, zero)]
        outs = _flash_multi(qs, [k_ref] * len(qs), v_ref, s_scr, p_scr, acc_scr)
        for r in range(subs):
            o = outs[2 * r] - lam * outs[2 * r + 1]
            o = _rmsnorm(o, g_ref[...]) * (1.0 - lambda_init)
            o_ref[base + r * tqs:base + (r + 1) * tqs, :] = o.astype(_BF16)


def _vt_block(vt):
    return pl.BlockSpec((None, None) + vt.shape[2:], lambda b, h, i: (b, h, 0, 0, 0))


def _diff_attn(q, k, vt, lq1, lk1, lq2, lk2, subln_g, batch, seq, lambda_init):
    t, d = q.shape
    tq, tqs, n_streams = _query_tiles(seq)
    tk = vt.shape[-1]
    nq = seq // tq
    qspec = pl.BlockSpec((tq, _LANES), lambda b, h, i: (b * nq + i, h))
    kspec = pl.BlockSpec((seq, _LANES), lambda b, h, i: (b, h))
    small = lambda n: pl.BlockSpec((1, n), lambda b, h, i: (0, 0))
    hd = _DIFF_HEAD_DIM
    return pl.pallas_call(
        functools.partial(_diff_attn_kernel, lambda_init=lambda_init),
        grid=(batch, _DIFF_HEADS, nq),
        in_specs=[qspec, kspec, _vt_block(vt), small(hd), small(hd), small(hd), small(hd),
                  small(2 * hd)],
        out_specs=qspec,
        out_shape=jax.ShapeDtypeStruct((t, d), _BF16),
        scratch_shapes=_flash_scratch(n_streams, tqs, tk, _LANES),
        compiler_params=_cparams("parallel", "parallel", "arbitrary"),
        name="diff_attn",
    )(q, k, vt, lq1.reshape(1, hd), lk1.reshape(1, hd), lq2.reshape(1, hd), lk2.reshape(1, hd),
      subln_g.reshape(1, 2 * hd))


def _mla_proj_kernel(x_ref, g_ref, wa_ref, gq_ref, wq_ref, gkv_ref, wk_ref, wvt_ref, c_ref, su_ref,
                     sd_ref, q_ref, k_ref, vt_ref, *, scale):
    h = _rmsnorm(x_ref[...], g_ref[...]).astype(_BF16)
    a = jnp.dot(h, wa_ref[...], preferred_element_type=_F32)
    c, su, sd = c_ref[...], su_ref[...], sd_ref[...]
    half = _MLA_ROPE // 2
    kv_lo = _MLA_Q_RANK + _MLA_KV_RANK
    cq = _rmsnorm(a[:, :_MLA_Q_RANK], gq_ref[...]).astype(_BF16)
    ckv = _rmsnorm(a[:, _MLA_Q_RANK:kv_lo], gkv_ref[...]).astype(_BF16)
    k_rope = _rope_lanes(a[:, kv_lo:kv_lo + _LANES], c, su, sd, half)
    qf = jnp.dot(cq, wq_ref[...], preferred_element_type=_F32)
    kf = jnp.dot(ckv, wk_ref[...], preferred_element_type=_F32)
    for j in range(_MLA_HEADS):
        lo, hi = j * _LANES, (j + 1) * _LANES
        q_ref[:, lo:hi] = (_rope_lanes(qf[:, lo:hi], c, su, sd, half) * scale).astype(_BF16)
        k_ref[:, lo:hi] = (kf[:, lo:hi] + k_rope).astype(_BF16)
    _store_vt(vt_ref, lax.dot_general(wvt_ref[...], ckv, _NT, preferred_element_type=_F32),
              _MLA_HEADS * _MLA_V // _LANES)


def _mla_weights(w_a, w_q_b, w_kv_b):
    d = w_a.shape[0]
    kv_lo = _MLA_Q_RANK + _MLA_KV_RANK
    z = lambda *s: jnp.zeros(s, _F32)
    pad = _LANES - _MLA_NOPE - _MLA_ROPE
    wa = jnp.concatenate([w_a[:, :kv_lo], z(d, _MLA_NOPE), w_a[:, kv_lo:], z(d, pad)], axis=1)
    wq = w_q_b.reshape(_MLA_Q_RANK, _MLA_HEADS, _MLA_NOPE + _MLA_ROPE)
    wq = jnp.concatenate([wq, z(_MLA_Q_RANK, _MLA_HEADS, pad)], axis=2)
    wkv = w_kv_b.reshape(_MLA_KV_RANK, _MLA_HEADS, _MLA_NOPE + _MLA_V)
    wk = jnp.concatenate([wkv[:, :, :_MLA_NOPE], z(_MLA_KV_RANK, _MLA_HEADS, _LANES - _MLA_NOPE)], axis=2)
    wvt = wkv[:, :, _MLA_NOPE:].reshape(_MLA_KV_RANK, -1).T
    return (wa.astype(_BF16), wq.reshape(_MLA_Q_RANK, -1).astype(_BF16),
            wk.reshape(_MLA_KV_RANK, -1).astype(_BF16), wvt.astype(_BF16))


def _mla_proj(x, g, w_a, gq, w_q_b, gkv, w_kv_b, batch, seq):
    t, d = x.shape
    tm = _key_chunk(seq)
    nseq = seq // tm
    c, su, sd = _rope_tables(seq, _LANES, _MLA_NOPE, _MLA_ROPE)
    wa, wq, wk, wvt = _mla_weights(w_a, w_q_b, w_kv_b)
    hq = _MLA_HEADS * _LANES
    tab = pl.BlockSpec((tm, _LANES), lambda i: (i % nseq, 0))
    full = lambda a: pl.BlockSpec(a.shape, lambda i: (0, 0))
    row = lambda n: pl.BlockSpec((tm, n), lambda i: (i, 0))
    g, gq, gkv = g.reshape(1, -1), gq.reshape(1, -1), gkv.reshape(1, -1)
    scale = (_MLA_NOPE + _MLA_ROPE) ** -0.5 * _LOG2E
    vt_shape, vt_spec = _vt_shape_spec(batch, _MLA_HEADS * _MLA_V // _LANES, seq, tm)
    return pl.pallas_call(
        functools.partial(_mla_proj_kernel, scale=scale),
        grid=(t // tm,),
        in_specs=[row(d), full(g), full(wa), full(gq), full(wq), full(gkv), full(wk), full(wvt),
                  tab, tab, tab],
        out_specs=[row(hq), row(hq), vt_spec],
        out_shape=[jax.ShapeDtypeStruct((t, hq), _BF16), jax.ShapeDtypeStruct((t, hq), _BF16), vt_shape],
        compiler_params=_cparams("parallel"),
        name="mla_proj",
    )(x, g, wa, gq, wq, gkv, wk, wvt, c, su, sd)


def _mla_attn_kernel(q_ref, ka_ref, kb_ref, v_ref, o_ref, s_scr, p_scr, acc_scr):
    tqs = s_scr.shape[-1]
    subs = s_scr.shape[0] // 2
    for base in range(0, q_ref.shape[0], subs * tqs):
        qs = []
        for r in range(subs):
            rows = slice(base + r * tqs, base + (r + 1) * tqs)
            qs += [q_ref[rows, :_LANES], q_ref[rows, _LANES:]]
        outs = _flash_multi(qs, [ka_ref, kb_ref] * subs, v_ref, s_scr, p_scr, acc_scr)
        lane = lax.broadcasted_iota(_I32, outs[0].shape, 1)
        for r in range(subs):
            o_ref[base + r * tqs:base + (r + 1) * tqs, :] = jnp.where(
                lane < _MLA_V, outs[2 * r], outs[2 * r + 1]).astype(_BF16)


def _mla_attn(q, k, vt, batch, seq):
    t = q.shape[0]
    tq, tqs, n_streams = _query_tiles(seq)
    tk = vt.shape[-1]
    nq = seq // tq
    return pl.pallas_call(
        _mla_attn_kernel,
        grid=(batch, _MLA_HEADS // 2, nq),
        in_specs=[pl.BlockSpec((tq, 2 * _LANES), lambda b, h, i: (b * nq + i, h)),
                  pl.BlockSpec((seq, _LANES), lambda b, h, i: (b, 2 * h)),
                  pl.BlockSpec((seq, _LANES), lambda b, h, i: (b, 2 * h + 1)),
                  _vt_block(vt)],
        out_specs=pl.BlockSpec((tq, _LANES), lambda b, h, i: (b * nq + i, h)),
        out_shape=jax.ShapeDtypeStruct((t, _MLA_HEADS * _MLA_V), _BF16),
        scratch_shapes=_flash_scratch(n_streams, tqs, tk, _LANES),
        compiler_params=_cparams("parallel", "parallel", "arbitrary"),
        name="mla_attn",
    )(q, k, k, vt)


def _oproj_router_kernel(o_ref, w_ref, x_ref, g_ref, rwh_ref, rwl_ref, x1_ref, hn_ref, aff_ref):
    x1 = x_ref[...] + jnp.dot(o_ref[...], w_ref[...], preferred_element_type=_F32)
    x1_ref[...] = x1
    hn = _rmsnorm(x1, g_ref[...])
    _to_row_tiles(hn_ref, hn)
    hh = hn.astype(_BF16)
    hl = (hn - hh.astype(_F32)).astype(_BF16)
    nt = (((1,), (1,)), ((), ()))
    rwh = rwh_ref[...]
    logits = (lax.dot_general(rwh, hh, nt, preferred_element_type=_F32)
              + lax.dot_general(rwh, hl, nt, preferred_element_type=_F32)
              + lax.dot_general(rwl_ref[...], hh, nt, preferred_element_type=_F32))
    e = jnp.exp(logits - jnp.max(logits, axis=0, keepdims=True))
    aff_ref[...] = e / jnp.sum(e, axis=0, keepdims=True)


def _oproj_router(o, w_o, x, g, router_w):
    t, d = x.shape
    tm = _tile(t, _TOKEN_TILE)
    ne = router_w.shape[1]
    rwt = router_w.T
    rwh = rwt.astype(_BF16)
    rwl = (rwt - rwh.astype(_F32)).astype(_BF16)
    row = pl.BlockSpec((tm, d), lambda i: (i, 0))
    full = lambda a: pl.BlockSpec(a.shape, lambda i: (0, 0))
    g = g.reshape(1, d)
    w = w_o.astype(_BF16)
    return pl.pallas_call(
        _oproj_router_kernel,
        grid=(t // tm,),
        in_specs=[row, full(w), row, full(g), full(rwh), full(rwl)],
        out_specs=[row, pl.BlockSpec((tm * _SUBLANES, _LANES), lambda i: (i, 0)),
                   pl.BlockSpec((ne, tm), lambda i: (0, i))],
        out_shape=[jax.ShapeDtypeStruct((t, d), _F32),
                   jax.ShapeDtypeStruct((t * d // _LANES, _LANES), _F32),
                   jax.ShapeDtypeStruct((ne, t), _F32)],
        compiler_params=_cparams("parallel"),
        name="oproj_router",
    )(o, w, x, g, rwh, rwl)


_MAGNITUDE_BITS = 31


def _select_kernel(aff_ref, idx_ref, gate_ref, base_ref, thr_scr, *, cap, n_experts, ns):
    rows = aff_ref.shape[0] // n_experts

    def iota(shape, dim):
        return lax.broadcasted_iota(_I32, shape, dim)

    sq = (_LANES, _LANES)
    upto = (iota(sq, 0) <= iota(sq, 1)).astype(_BF16)
    upto_t = (iota(sq, 1) <= iota(sq, 0)).astype(_BF16)
    above = (iota((rows, rows), 1) < iota((rows, rows), 0)).astype(_BF16)
    capf = jnp.float32(cap)
    row_id = iota((rows, ns), 0).astype(_F32)
    lane_id = iota((_LANES, ns), 0).astype(_F32)
    slot_id = iota((1, ns), 1).astype(_F32)

    def lanes(col):
        return jnp.broadcast_to(col, (rows, _LANES))

    def rows_before(tot):
        return jnp.dot(above, lanes(tot).astype(_BF16), preferred_element_type=_F32)[:, :1]

    def step(i, thrs):
        bit = lax.shift_left(jnp.int32(1),
                             jnp.int32(_MAGNITUDE_BITS - 1) - lax.convert_element_type(i, _I32))
        out = []
        for e in range(n_experts):
            bits = pltpu.bitcast(aff_ref[e * rows:(e + 1) * rows, :], _I32)
            cand = thrs[e] | bit
            count = jnp.sum(jnp.where(bits >= cand, 1.0, 0.0), keepdims=True)
            out.append(jnp.where(count >= capf, cand, thrs[e]))
        return tuple(out)

    thrs = lax.fori_loop(0, _MAGNITUDE_BITS, step,
                         tuple(jnp.zeros((1, 1), _I32) for _ in range(n_experts)))
    for e in range(n_experts):
        thr_scr[e:e + 1, :] = jnp.broadcast_to(thrs[e], (1, _LANES))

    def expert(e, carry):
        r0 = pl.multiple_of(e * rows, rows)
        aff = aff_ref[pl.ds(r0, rows), :]
        bits = pltpu.bitcast(aff, _I32)
        thr = thr_scr[pl.ds(e, 1), :][:, :1]
        gt = bits > thr
        eq = jnp.where(bits == thr, 1.0, 0.0)
        need = capf - jnp.sum(jnp.where(gt, 1.0, 0.0), keepdims=True)
        rank = (rows_before(jnp.sum(eq, axis=-1, keepdims=True))
                + jnp.dot(eq.astype(_BF16), upto, preferred_element_type=_F32) - eq)
        sel = jnp.where(gt | ((eq > 0.0) & (rank < need)), 1.0, 0.0)

        tot = jnp.sum(sel, axis=-1, keepdims=True)
        base = rows_before(tot)
        cum = base + tot
        base_ref[pl.ds(r0, rows), :] = lanes(base).astype(_I32)
        incl_t = jnp.dot(upto_t, sel.T.astype(_BF16), preferred_element_type=_F32).astype(_BF16)
        aff_t = aff.T
        a0 = aff_t.astype(_BF16)
        a1 = (aff_t - a0.astype(_F32)).astype(_BF16)
        a2 = (aff_t - a0.astype(_F32) - a1.astype(_F32)).astype(_BF16)
        for c in range(cap // ns):
            j = slot_id + float(c * ns)
            rj = jnp.sum(jnp.where(cum <= j, 1.0, 0.0), axis=0, keepdims=True)
            hit = row_id == rj
            q = j - jnp.sum(jnp.where(hit, base, 0.0), axis=0, keepdims=True)
            onehot = jnp.where(hit, 1.0, 0.0).astype(_BF16)
            counts = jnp.dot(incl_t, onehot, preferred_element_type=_F32)
            lpos = jnp.sum(jnp.where(counts <= q, 1.0, 0.0), axis=0, keepdims=True)
            arow = (jnp.dot(a0, onehot, preferred_element_type=_F32)
                    + jnp.dot(a1, onehot, preferred_element_type=_F32)
                    + jnp.dot(a2, onehot, preferred_element_type=_F32))
            gate = jnp.sum(jnp.where(lane_id == lpos, arow, 0.0), axis=0, keepdims=True)
            idx_ref[e, pl.ds(c, 1), :] = (rj * float(_LANES) + lpos).astype(_I32)
            gate_ref[e, pl.ds(c, 1), :] = gate
        return carry

    lax.fori_loop(0, n_experts, expert, 0)


def _select(aff_t, cap):
    ne, t = aff_t.shape
    rows = t // _LANES
    ns = _tile(cap, _SLOT_TILE)
    aff2 = aff_t.reshape(ne * rows, _LANES)
    listing = jax.ShapeDtypeStruct((ne, cap // ns, ns), _I32)
    idx, gates, base = pl.pallas_call(
        functools.partial(_select_kernel, cap=cap, n_experts=ne, ns=ns),
        out_shape=[listing, jax.ShapeDtypeStruct(listing.shape, _F32),
                   jax.ShapeDtypeStruct(aff2.shape, _I32)],
        scratch_shapes=[pltpu.VMEM((ne, _LANES), _I32)],
        compiler_params=pltpu.CompilerParams(vmem_limit_bytes=_VMEM_LIMIT),
        name="expert_select",
    )(aff2)
    return idx.reshape(ne, cap), gates.reshape(ne, cap), base.reshape(ne, rows, _LANES)[:, :, 0]


_SUBLANES = 8


def _to_row_tiles(dst_ref, x):
    n = x.shape[0]
    for j in range(x.shape[1] // _LANES):
        dst_ref[pl.ds(j, n, stride=_SUBLANES), :] = x[:, j * _LANES:(j + 1) * _LANES]


def _from_row_tiles(src_ref, n):
    return jnp.concatenate([src_ref[pl.ds(j, n, stride=_SUBLANES), :] for j in range(_SUBLANES)],
                           axis=1)


def _ffn_kernel(idx_ref, nxt_ref, hn_hbm, gate_ref, wg_ref, wu_ref, wd_ref, ye_ref, xbuf, sems,
                *, f_chunk):
    tc = xbuf.shape[1] // _SUBLANES
    n_tiles = pl.num_programs(1)
    step = pl.program_id(0) * n_tiles + pl.program_id(1)
    last = pl.num_programs(0) * n_tiles - 1
    slot = step % 2

    def row_copy(j, t, buf):
        src = hn_hbm.at[pl.ds(pl.multiple_of(t * _SUBLANES, _SUBLANES), _SUBLANES)]
        dst = xbuf.at[buf, pl.ds(pl.multiple_of(j * _SUBLANES, _SUBLANES), _SUBLANES)]
        return pltpu.make_async_copy(src, dst, sems.at[buf])

    def gather(ids_ref, buf, unroll):
        def issue(j, carry):
            row_copy(j, ids_ref[0, 0, j], buf).start()
            return carry
        lax.fori_loop(0, tc, issue, 0, unroll=unroll)

    @pl.when(step == 0)
    def _():
        gather(idx_ref, 0, 8)

    @pl.when(step < last)
    def _():
        gather(nxt_ref, 1 - slot, True)

    def drain(j, carry):
        row_copy(j, 0, slot).wait()
        return carry

    lax.fori_loop(0, tc, drain, 0, unroll=True)
    x = _from_row_tiles(xbuf.at[slot], tc).astype(_BF16)
    d_ff = wg_ref.shape[-1]
    y = jnp.zeros((tc, wd_ref.shape[-1]), _F32)
    for f in range(0, d_ff, f_chunk):
        g = jnp.dot(x, wg_ref[0, :, f:f + f_chunk], preferred_element_type=_F32)
        u = jnp.dot(x, wu_ref[0, :, f:f + f_chunk], preferred_element_type=_F32)
        hid = (g * jax.nn.sigmoid(g) * u).astype(_BF16)
        y = y + jnp.dot(hid, wd_ref[0, f:f + f_chunk, :], preferred_element_type=_F32)
    _to_row_tiles(ye_ref, y * gate_ref[0])


def _expert_ffn(hn, idx, gates, wg, wu, wd, layer):
    ne, cap = idx.shape
    d, d_ff = wg.shape[-2:]
    tc = _tile(cap, _SLOT_TILE)
    nc = cap // tc
    last = ne * nc - 1
    ids = idx.reshape(ne * nc, 1, tc)
    return pl.pallas_call(
        functools.partial(_ffn_kernel, f_chunk=_tile(d_ff, _FF_CHUNK)),
        grid=(ne, nc),
        in_specs=[pl.BlockSpec((1, 1, tc), lambda e, j: (e * nc + j, 0, 0), memory_space=pltpu.SMEM),
                  pl.BlockSpec((1, 1, tc), lambda e, j: (jnp.minimum(e * nc + j + 1, last), 0, 0),
                               memory_space=pltpu.SMEM),
                  pl.BlockSpec(memory_space=pl.ANY),
                  pl.BlockSpec((1, tc, 1), lambda e, j: (e, j, 0)),
                  pl.BlockSpec((None, 1, d, d_ff), lambda e, j: (layer, e, 0, 0)),
                  pl.BlockSpec((None, 1, d, d_ff), lambda e, j: (layer, e, 0, 0)),
                  pl.BlockSpec((None, 1, d_ff, d), lambda e, j: (layer, e, 0, 0))],
        out_specs=pl.BlockSpec((tc * _SUBLANES, _LANES), lambda e, j: (e * nc + j, 0)),
        out_shape=jax.ShapeDtypeStruct((ne * cap * d // _LANES, _LANES), _F32),
        scratch_shapes=[pltpu.VMEM((2, tc * _SUBLANES, _LANES), _F32), pltpu.SemaphoreType.DMA((2,))],
        compiler_params=_cparams("arbitrary", "arbitrary"),
        name="expert_ffn",
    )(ids, ids, hn, gates.reshape(ne, cap, 1), wg, wu, wd)


_ROW_CHUNK = 16
_RMW_GROUP = 8


def _div(x, n):
    if n & (n - 1) == 0:
        return lax.shift_right_logical(x, jnp.int32(n.bit_length() - 1))
    return lax.div(x, jnp.int32(n))


def _combine_kernel(lo_ref, x_ref, ye_hbm, *rest, n_experts, cap, idx_block, final_norm):
    idx_refs, rest = rest[:n_experts], rest[n_experts:]
    if final_norm:
        g_ref, rest = rest[0], rest[1:]
    out_ref, acc, stage, sems = rest
    tb = x_ref.shape[0]
    region = stage.shape[1] // (n_experts * _SUBLANES)
    b = pl.program_id(0)
    slot = b % 2

    def tile_of(row):
        return pl.ds(pl.multiple_of(row * _SUBLANES, _SUBLANES), _SUBLANES)

    def for_each_chunk(blk, buf, fn):
        size = _ROW_CHUNK * _SUBLANES
        for e in range(n_experts):
            lo, hi = lo_ref[e, blk], lo_ref[e, blk + 1]
            c0 = _div(lo, _ROW_CHUNK)

            def one(c, carry):
                src = pl.multiple_of((e * cap + c * _ROW_CHUNK) * _SUBLANES, size)
                dst = pl.multiple_of((e * region + (c - c0) * _ROW_CHUNK) * _SUBLANES, size)
                fn(pltpu.make_async_copy(ye_hbm.at[pl.ds(src, size)],
                                         stage.at[buf, pl.ds(dst, size)], sems.at[buf]))
                return carry

            lax.fori_loop(c0, _div(hi + (_ROW_CHUNK - 1), _ROW_CHUNK), one, 0)

    @pl.when(b == 0)
    def _():
        for_each_chunk(0, 0, lambda cp: cp.start())

    @pl.when(b + 1 < pl.num_programs(0))
    def _():
        for_each_chunk(b + 1, 1 - slot, lambda cp: cp.start())

    _to_row_tiles(acc, x_ref[...])
    for_each_chunk(b, slot, lambda cp: cp.wait())
    for e in range(n_experts):
        lo, hi = lo_ref[e, b], lo_ref[e, b + 1]
        first = _div(lo, idx_block) * idx_block
        row0 = e * region - _div(lo, _ROW_CHUNK) * _ROW_CHUNK
        ids = idx_refs[e]

        def token(j):
            return ids[0, 0, 0, j - first] - b * tb

        def add_rows(j0, count):
            toks = [tile_of(token(j0 + i)) for i in range(count)]
            sums = [acc[toks[i], :] + stage[slot, tile_of(row0 + j0 + i), :] for i in range(count)]
            for i in range(count):
                acc[toks[i], :] = sums[i]

        def group(g, carry):
            add_rows(lo + g * _RMW_GROUP, _RMW_GROUP)
            return carry

        def single(j, carry):
            add_rows(j, 1)
            return carry

        n_groups = _div(hi - lo, _RMW_GROUP)
        lax.fori_loop(0, n_groups, group, 0)
        lax.fori_loop(lo + n_groups * _RMW_GROUP, hi, single, 0)

    y = _from_row_tiles(acc, tb)
    out_ref[...] = _rmsnorm(y, g_ref[...]) if final_norm else y


def _combine(x1, ye, idx, before, final_g=None):
    t, d = x1.shape
    ne, cap = idx.shape
    assert d == _SUBLANES * _LANES
    tb = _tile(t, _COMBINE_BLOCK)
    nb = t // tb
    assert cap % _ROW_CHUNK == 0
    idx_block = min(tb, cap)
    n_ib = cap // idx_block
    lo = jnp.concatenate([before[:, ::tb // _LANES], jnp.full((ne, 1), cap, _I32)], axis=1)
    idx3 = idx.reshape(ne, n_ib, idx_block)
    windows = jnp.concatenate([idx3, jnp.concatenate([idx3[:, 1:], idx3[:, -1:]], axis=1)], axis=2)
    windows = windows.reshape(ne, n_ib, 1, 2 * idx_block)

    def idx_spec(e):
        return pl.BlockSpec((1, 1, 1, 2 * idx_block),
                            lambda b, lo_ref: (e, _div(lo_ref[e, b], idx_block), 0, 0),
                            memory_space=pltpu.SMEM)

    region = tb + _ROW_CHUNK
    row = pl.BlockSpec((tb, d), lambda b, lo_ref: (b, 0))
    norm_specs, norm_args = [], []
    if final_g is not None:
        norm_specs, norm_args = [pl.BlockSpec((1, d), lambda b, lo_ref: (0, 0))], [final_g.reshape(1, d)]
    grid_spec = pltpu.PrefetchScalarGridSpec(
        num_scalar_prefetch=1,
        grid=(nb,),
        in_specs=[row, pl.BlockSpec(memory_space=pl.ANY)] + [idx_spec(e) for e in range(ne)]
                 + norm_specs,
        out_specs=row,
        scratch_shapes=[pltpu.VMEM((tb * _SUBLANES, _LANES), _F32),
                        pltpu.VMEM((2, ne * region * _SUBLANES, _LANES), _F32),
                        pltpu.SemaphoreType.DMA((2,))])
    return pl.pallas_call(
        functools.partial(_combine_kernel, n_experts=ne, cap=cap, idx_block=idx_block,
                          final_norm=final_g is not None),
        grid_spec=grid_spec,
        out_shape=jax.ShapeDtypeStruct((t, d), _F32),
        compiler_params=_cparams("arbitrary"),
        name="moe_combine",
    )(lo, x1, ye, *([windows] * ne), *norm_args)


def _moe(x1, hn, aff_t, wg, wu, wd, layer, final_g):
    t, d = x1.shape
    ne = aff_t.shape[0]
    cap = max(1, _CAPACITY_FACTOR * t // ne)
    idx, gates, before = _select(aff_t, cap)
    ye = _expert_ffn(hn, idx, gates, wg, wu, wd, layer)
    return _combine(x1, ye, idx, before, final_g)


def _trunk(x, p):
    batch, seq, d = x.shape
    x = x.reshape(batch * seq, d)
    depth = p["ffn_norm_g"].shape[0]
    for i in range(depth):
        j = i // _N_MIXERS
        if i % _N_MIXERS == 0:
            lambda_init = 0.8 - 0.6 * float(np.exp(-0.3 * i))
            q, k, v = _diff_qkv(x, p["diff_norm_g"][j], p["diff_w_qkv"][j], batch, seq)
            o = _diff_attn(q, k, v, p["diff_lambda_q1"][j], p["diff_lambda_k1"][j],
                           p["diff_lambda_q2"][j], p["diff_lambda_k2"][j], p["diff_subln_g"][j],
                           batch, seq, lambda_init)
            w_o = p["diff_w_o"][j]
        else:
            q, k, v = _mla_proj(x, p["mla_norm_g"][j], p["mla_w_a"][j], p["mla_q_norm_g"][j],
                                p["mla_w_q_b"][j], p["mla_kv_norm_g"][j], p["mla_w_kv_b"][j],
                                batch, seq)
            o = _mla_attn(q, k, v, batch, seq)
            w_o = p["mla_w_o"][j]
        x1, hn, aff_t = _oproj_router(o, w_o, x, p["ffn_norm_g"][i], p["router_w"][i])
        x = _moe(x1, hn, aff_t, p["w_gate_bf16"], p["w_up_bf16"], p["w_down_bf16"], i,
                 p["final_norm_g"] if i == depth - 1 else None)
    return x.reshape(batch, seq, d)


def kernel(x_prompt, x_sample, diff_norm_g, diff_w_qkv, diff_lambda_q1, diff_lambda_k1, diff_lambda_q2,
           diff_lambda_k2, diff_subln_g, diff_w_o, mla_norm_g, mla_w_a, mla_q_norm_g, mla_w_q_b,
           mla_kv_norm_g, mla_w_kv_b, mla_w_o, ffn_norm_g, router_w, w_gate, w_up, w_down, final_norm_g):
    p = dict(diff_norm_g=diff_norm_g, diff_w_qkv=diff_w_qkv, diff_lambda_q1=diff_lambda_q1,
             diff_lambda_k1=diff_lambda_k1, diff_lambda_q2=diff_lambda_q2, diff_lambda_k2=diff_lambda_k2,
             diff_subln_g=diff_subln_g, diff_w_o=diff_w_o, mla_norm_g=mla_norm_g, mla_w_a=mla_w_a,
             mla_q_norm_g=mla_q_norm_g, mla_w_q_b=mla_w_q_b, mla_kv_norm_g=mla_kv_norm_g,
             mla_w_kv_b=mla_w_kv_b, mla_w_o=mla_w_o, ffn_norm_g=ffn_norm_g, router_w=router_w,
             final_norm_g=final_norm_g, w_gate_bf16=w_gate.astype(_BF16), w_up_bf16=w_up.astype(_BF16),
             w_down_bf16=w_down.astype(_BF16))
    return _trunk(x_prompt, p), _trunk(x_sample, p)
```
